```python
import jax, jax.numpy as jnp
from jax import lax
import numpy as np

D_MODEL = 1024
BATCH = 1
SEQ = 16384
DEPTH = 2
DEC_BATCH = 2
DEC_SEQ = 16384
PAST_LEN = 128

GRID_W = 64
BLOCK = 128
HEAD_DIM = 64
A_HEADS = 8
A_KV_HEADS = 2
B_HEADS = 8
B_KV_HEADS = 2
WINDOW = 128
IN_COLS = (A_HEADS + 2 * A_KV_HEADS + B_HEADS + 2 * B_KV_HEADS) * HEAD_DIM
MIX_WIDTH = (A_HEADS + B_HEADS) * HEAD_DIM
C_HEADS = 16
Q_LORA = 384
KV_LORA = 256
NOPE_DIM = 64
ROPE_DIM = 32
V_DIM = 64
C_DOWN_COLS = Q_LORA + KV_LORA + ROPE_DIM
D_FF = 2816
CONV_W = 3
ROPE_THETA = 10000.0
EPS = 1e-6
N_EVEN = (DEPTH + 1) // 2
N_ODD = DEPTH // 2

kernel_name = 'hybrid_axial_window_mla_encoder'


def _rms_norm(x, g):
    xf = x.astype(jnp.float32)
    y = xf * lax.rsqrt(jnp.mean(xf * xf, axis=-1, keepdims=True) + EPS)
    return (y * g.astype(jnp.float32)).astype(x.dtype)


def _rope_angles(pos, dim):
    inv_freq = ROPE_THETA ** (-jnp.arange(0, dim, 2, dtype=jnp.float32) / dim)
    ang = pos.astype(jnp.float32)[:, None] * inv_freq[None, :]
    return jnp.cos(ang), jnp.sin(ang)


def _apply_rope(x, cos, sin):
    xf = x.astype(jnp.float32)
    x1, x2 = jnp.split(xf, 2, axis=-1)
    c = cos[:, None, :]
    sn = sin[:, None, :]
    return jnp.concatenate([x1 * c - x2 * sn, x2 * c + x1 * sn], axis=-1).astype(x.dtype)


def _axial_rope(x, cos_r, sin_r, cos_c, sin_c):
    half = x.shape[-1] // 2
    return jnp.concatenate([_apply_rope(x[..., :half], cos_r, sin_r),
                            _apply_rope(x[..., half:], cos_c, sin_c)], axis=-1)


def _dense_attention(q, k, v, scale):
    b, s, hq, dk = q.shape
    hkv = k.shape[2]
    g = hq // hkv
    dv = v.shape[-1]
    nblk = s // BLOCK
    qb = jnp.moveaxis(q.reshape(b, nblk, BLOCK, hkv, g, dk), 1, 0)

    def one_block(qi):
        sc = jnp.einsum('bqkgd,bskd->bkgqs', qi, k, preferred_element_type=jnp.float32) * scale
        p = jax.nn.softmax(sc, axis=-1)
        return jnp.einsum('bkgqs,bskd->bqkgd', p.astype(v.dtype), v)

    o = lax.map(one_block, qb)
    return jnp.moveaxis(o, 0, 1).reshape(b, s, hq, dv)


def _window_sink_attention(q, k, v, sink, slopes):
    b, s, hq, d = q.shape
    hkv = k.shape[2]
    g = hq // hkv
    nblk = s // BLOCK
    qb = q.reshape(b, nblk, BLOCK, hkv, g, d)

    def band(z):
        zp = jnp.pad(z, ((0, 0), (BLOCK, BLOCK), (0, 0), (0, 0)))
        zp = zp.reshape(b, nblk + 2, BLOCK, z.shape[2], z.shape[3])
        return jnp.concatenate([zp[:, :-2], zp[:, 1:-1], zp[:, 2:]], axis=2)

    kb = band(k)
    vb = band(v)
    sc = jnp.einsum('bnqkgd,bnskd->bnkgqs', qb, kb, preferred_element_type=jnp.float32) * (d ** -0.5)
    dist = jnp.abs(jnp.arange(3 * BLOCK)[None, :] - BLOCK - jnp.arange(BLOCK)[:, None])
    kpos = (jnp.arange(nblk)[:, None] - 1) * BLOCK + jnp.arange(3 * BLOCK)[None, :]
    valid = (dist <= WINDOW)[None] & ((kpos >= 0) & (kpos < s))[:, None, :]
    bias = -slopes.reshape(hkv, g)[:, :, None, None] * dist.astype(jnp.float32)
    sc = jnp.where(valid[None, :, None, None], sc + bias, -jnp.inf)
    sink_l = sink.astype(jnp.float32).reshape(1, 1, hkv, g, 1, 1)
    m = jnp.maximum(jnp.max(sc, axis=-1, keepdims=True), sink_l)
    e = jnp.exp(sc - m)
    p = e / (jnp.sum(e, axis=-1, keepdims=True) + jnp.exp(sink_l - m))
    o = jnp.einsum('bnkgqs,bnskd->bnqkgd', p.astype(v.dtype), vb)
    return o.reshape(b, s, hq, d)


def _even_mixer(h, w_in, q_gain, k_gain, sink, w_out, axial, slopes):
    b, s, _ = h.shape
    sizes = [A_HEADS * HEAD_DIM, A_KV_HEADS * HEAD_DIM, A_KV_HEADS * HEAD_DIM,
             B_HEADS * HEAD_DIM, B_KV_HEADS * HEAD_DIM, B_KV_HEADS * HEAD_DIM]
    cuts = [int(c) for c in np.cumsum(sizes)[:-1]]
    qa, ka, va, qb, kb, vb = jnp.split(h @ w_in, cuts, axis=-1)
    qa = _axial_rope(_rms_norm(qa.reshape(b, s, A_HEADS, HEAD_DIM), q_gain), *axial)
    ka = _axial_rope(_rms_norm(ka.reshape(b, s, A_KV_HEADS, HEAD_DIM), k_gain), *axial)
    va = va.reshape(b, s, A_KV_HEADS, HEAD_DIM)
    oa = _dense_attention(qa, ka, va, HEAD_DIM ** -0.5)
    qb = qb.reshape(b, s, B_HEADS, HEAD_DIM)
    kb = kb.reshape(b, s, B_KV_HEADS, HEAD_DIM)
    vb = vb.reshape(b, s, B_KV_HEADS, HEAD_DIM)
    ob = _window_sink_attention(qb, kb, vb, sink, slopes)
    o = jnp.concatenate([oa.reshape(b, s, A_HEADS * HEAD_DIM), ob.reshape(b, s, B_HEADS * HEAD_DIM)], axis=-1)
    return o @ w_out


def _mla(h, w_down, q_gain, kv_gain, w_uq, w_ukv, w_out, cos, sin):
    b, s, _ = h.shape
    cq, ckv, k_rope = jnp.split(h @ w_down, [Q_LORA, Q_LORA + KV_LORA], axis=-1)
    q = (_rms_norm(cq, q_gain) @ w_uq).reshape(b, s, C_HEADS, NOPE_DIM + ROPE_DIM)
    q_nope, q_rope = jnp.split(q, [NOPE_DIM], axis=-1)
    kv = (_rms_norm(ckv, kv_gain) @ w_ukv).reshape(b, s, C_HEADS, NOPE_DIM + V_DIM)
    k_nope, v = jnp.split(kv, [NOPE_DIM], axis=-1)
    k_rope = _apply_rope(k_rope[:, :, None, :], cos, sin)
    q = jnp.concatenate([q_nope, _apply_rope(q_rope, cos, sin)], axis=-1)
    k = jnp.concatenate([k_nope, jnp.broadcast_to(k_rope, (b, s, C_HEADS, ROPE_DIM))], axis=-1)
    o = _dense_attention(q, k, v, (NOPE_DIM + ROPE_DIM) ** -0.5)
    return o.reshape(b, s, C_HEADS * V_DIM) @ w_out


def _conv_glu(h, w_up, conv_w, conv_b, w_down):
    s = h.shape[1]
    gate, val = jnp.split(h @ w_up, 2, axis=-1)
    pad = CONV_W // 2
    gp = jnp.pad(gate, ((0, 0), (pad, pad), (0, 0)))
    gc = conv_b
    for j in range(CONV_W):
        gc = gc + gp[:, j:j + s] * conv_w[j]
    return (jax.nn.silu(gc) * val) @ w_down


def _trunk(x, norm_mix, norm_ffn, norm_final, e_w_in, e_q_gain, e_k_gain, e_sink, e_w_out,
           o_w_down, o_q_gain, o_kv_gain, o_w_uq, o_w_ukv, o_w_out,
           f_w_up, f_conv_w, f_conv_b, f_w_down):
    s = x.shape[1]
    rows = s // GRID_W
    row = jnp.repeat(jnp.arange(rows), GRID_W)
    col = jnp.tile(jnp.arange(GRID_W), rows)
    cos_r, sin_r = _rope_angles(row, HEAD_DIM // 2)
    cos_c, sin_c = _rope_angles(col, HEAD_DIM // 2)
    axial = (cos_r, sin_r, cos_c, sin_c)
    cos_t, sin_t = _rope_angles(jnp.arange(s), ROPE_DIM)
    slopes = 2.0 ** (-8.0 * jnp.arange(1, B_HEADS + 1, dtype=jnp.float32) / B_HEADS)
    for l in range(DEPTH):
        h = _rms_norm(x, norm_mix[l])
        if l % 2 == 0:
            i = l // 2
            x = x + _even_mixer(h, e_w_in[i], e_q_gain[i], e_k_gain[i], e_sink[i], e_w_out[i], axial, slopes)
        else:
            i = l // 2
            x = x + _mla(h, o_w_down[i], o_q_gain[i], o_kv_gain[i], o_w_uq[i], o_w_ukv[i], o_w_out[i], cos_t, sin_t)
        h = _rms_norm(x, norm_ffn[l])
        x = x + _conv_glu(h, f_w_up[l], f_conv_w[l], f_conv_b[l], f_w_down[l])
    return _rms_norm(x, norm_final)


def setup_inputs(seed: int = 0) -> dict:
    key = jax.random.key(seed)
    ks = jax.random.split(key, 20)
    f32 = jnp.float32

    def w(k, shape, fan_in):
        return jax.random.normal(k, shape, f32) * (fan_in ** -0.5)

    def gain(k, shape):
        return 1.0 + 0.02 * jax.random.normal(k, shape, f32)

    return {
        'x_prompt': jax.random.normal(ks[0], (BATCH, SEQ, D_MODEL), f32),
        'x_sample': jax.random.normal(ks[1], (DEC_BATCH, DEC_SEQ, D_MODEL), f32),
        'norm_mix': gain(ks[2], (DEPTH, D_MODEL)),
        'norm_ffn': gain(ks[3], (DEPTH, D_MODEL)),
        'norm_final': gain(ks[4], (D_MODEL,)),
        'e_w_in': w(ks[5], (N_EVEN, D_MODEL, IN_COLS), D_MODEL),
        'e_q_gain': gain(ks[6], (N_EVEN, HEAD_DIM)),
        'e_k_gain': gain(ks[7], (N_EVEN, HEAD_DIM)),
        'e_sink': 0.5 * jax.random.normal(ks[8], (N_EVEN, B_HEADS), f32),
        'e_w_out': w(ks[9], (N_EVEN, MIX_WIDTH, D_MODEL), MIX_WIDTH),
        'o_w_down': w(ks[10], (N_ODD, D_MODEL, C_DOWN_COLS), D_MODEL),
        'o_q_gain': gain(ks[11], (N_ODD, Q_LORA)),
        'o_kv_gain': gain(ks[12], (N_ODD, KV_LORA)),
        'o_w_uq': w(ks[13], (N_ODD, Q_LORA, C_HEADS * (NOPE_DIM + ROPE_DIM)), Q_LORA),
        'o_w_ukv': w(ks[14], (N_ODD, KV_LORA, C_HEADS * (NOPE_DIM + V_DIM)), KV_LORA),
        'o_w_out': w(ks[15], (N_ODD, C_HEADS * V_DIM, D_MODEL), C_HEADS * V_DIM),
        'f_w_up': w(ks[16], (DEPTH, D_MODEL, 2 * D_FF), D_MODEL),
        'f_conv_w': w(ks[17], (DEPTH, CONV_W, D_FF), CONV_W),
        'f_conv_b': 0.01 * jax.random.normal(ks[18], (DEPTH, D_FF), f32),
        'f_w_down': w(ks[19], (DEPTH, D_FF, D_MODEL), D_FF),
    }


def reference(x_prompt, x_sample, norm_mix, norm_ffn, norm_final, e_w_in, e_q_gain, e_k_gain, e_sink,
              e_w_out, o_w_down, o_q_gain, o_kv_gain, o_w_uq, o_w_ukv, o_w_out,
              f_w_up, f_conv_w, f_conv_b, f_w_down):
    y_prompt = _trunk(x_prompt, norm_mix, norm_ffn, norm_final, e_w_in, e_q_gain, e_k_gain, e_sink, e_w_out,
                      o_w_down, o_q_gain, o_kv_gain, o_w_uq, o_w_ukv, o_w_out,
                      f_w_up, f_conv_w, f_conv_b, f_w_down)
    y_sample = _trunk(x_sample, norm_mix, norm_ffn, norm_final, e_w_in, e_q_gain, e_k_gain, e_sink, e_w_out,
                      o_w_down, o_q_gain, o_kv_gain, o_w_uq, o_w_ukv, o_w_out,
                      f_w_up, f_conv_w, f_conv_b, f_w_down)
    return (y_prompt, y_sample)
```

```python
import functools
import math

import numpy as np
import jax
import jax.numpy as jnp
from jax import lax
from jax.experimental import pallas as pl
from jax.experimental.pallas import tpu as pltpu

D_MODEL = 1024
GRID_W = 64
HEAD_DIM = 64
A_HEADS = 8
A_KV_HEADS = 2
B_HEADS = 8
B_KV_HEADS = 2
WINDOW = 128
C_HEADS = 16
Q_LORA = 384
KV_LORA = 256
NOPE_DIM = 64
ROPE_DIM = 32
V_DIM = 64
D_FF = 2816
CONV_W = 3
ROPE_THETA = 10000.0
EPS = 1e-6

LANES = 128
LOG2E = math.log2(math.e)
VMEM_LIMIT = 56 * 1024 * 1024

F32 = jnp.float32
BF16 = jnp.bfloat16

QK_A = (A_HEADS + A_KV_HEADS) * HEAD_DIM


def _params(*sem):
    return pltpu.CompilerParams(dimension_semantics=sem, vmem_limit_bytes=VMEM_LIMIT)


def _const_spec(shape):
    nd = len(shape)
    return pl.BlockSpec(shape, lambda *_: (0,) * nd, pipeline_mode=pl.Buffered(1))


def _rms(x, gain):
    ms = jnp.mean(x * x, axis=-1, keepdims=True)
    return x * lax.rsqrt(ms + EPS) * gain


def _lane_ids(shape):
    return lax.broadcasted_iota(jnp.int32, shape, len(shape) - 1)


def _low_half(x):
    return jnp.where(_lane_ids(x.shape) < HEAD_DIM, x, 0.0)


def _swap_halves(x):
    return pltpu.roll(x, HEAD_DIM, axis=1)


def _with_ones_column(v):
    lane = _lane_ids(v.shape)
    return jnp.where(lane < V_DIM, v, jnp.where(lane == V_DIM, 1.0, 0.0))


def _inproj_even_kernel(x_ref, g_ref, w_ref, avg_ref, gq_ref, gr_ref, cos_ref, sin_ref,
                        qa_ref, ka_ref, va_ref, qb_ref, kb_ref, vb_ref, *, qb_scale):
    h = _rms(x_ref[...], g_ref[...]).astype(BF16)
    y = jnp.dot(h, w_ref[...], preferred_element_type=F32)
    yqk = y[:, :QK_A]
    yrot = y[:, QK_A:2 * QK_A]
    sq = yqk * yqk
    hi = sq.astype(BF16)
    lo = (sq - hi.astype(F32)).astype(BF16)
    msq = (jnp.dot(hi, avg_ref[...], preferred_element_type=F32)
           + jnp.dot(lo, avg_ref[...], preferred_element_type=F32))
    r = lax.rsqrt(msq + EPS)
    cos = cos_ref[...]
    sin = sin_ref[...]
    for c in range(QK_A // LANES):
        sl = slice(c * LANES, (c + 1) * LANES)
        a = (yqk[:, sl] * (gq_ref[:, sl] * cos) + yrot[:, sl] * (gr_ref[:, sl] * sin)) * r[:, sl]
        even = _low_half(a).astype(BF16)
        odd = _low_half(_swap_halves(a)).astype(BF16)
        if c < A_HEADS // 2:
            qa_ref[2 * c] = even
            qa_ref[2 * c + 1] = odd
        else:
            ka_ref[0] = even
            ka_ref[1] = odd
    base = 2 * QK_A
    v = y[:, base:base + LANES]
    va_ref[0] = _with_ones_column(v).astype(BF16)
    va_ref[1] = _with_ones_column(_swap_halves(v)).astype(BF16)
    base += LANES
    for c in range(B_HEADS // 2):
        a = y[:, base + c * LANES: base + (c + 1) * LANES] * qb_scale
        qb_ref[2 * c] = _low_half(a).astype(BF16)
        qb_ref[2 * c + 1] = _low_half(_swap_halves(a)).astype(BF16)
    base += B_HEADS * HEAD_DIM
    kk = y[:, base:base + LANES]
    kb_ref[0] = _low_half(kk).astype(BF16)
    kb_ref[1] = _low_half(_swap_halves(kk)).astype(BF16)
    base += LANES
    v = y[:, base:base + LANES]
    vb_ref[0] = _with_ones_column(v).astype(BF16)
    vb_ref[1] = _with_ones_column(_swap_halves(v)).astype(BF16)


def _rot_half_perm(n_cols, half):
    col = np.arange(n_cols)
    first = (col % (2 * half)) < half
    perm = np.where(first, col + half, col - half)
    sign = np.where(first, -1.0, 1.0).astype(np.float32)
    return perm, sign


def _inproj_even(x, g_mix, w_in, q_gain, k_gain, cos128, sin128, *, tm):
    b, s, d = x.shape
    q_scale = HEAD_DIM ** -0.5 * LOG2E
    w_qk = w_in[:, :QK_A]
    perm, sign = _rot_half_perm(QK_A, HEAD_DIM // 4)
    w_rot = w_qk[:, perm] * sign
    w_all = jnp.concatenate([w_qk, w_rot, w_in[:, QK_A:]], axis=1).astype(BF16)
    gains = jnp.concatenate([jnp.tile(q_gain * q_scale, A_HEADS), jnp.tile(k_gain, A_KV_HEADS)])
    gq = gains.reshape(1, QK_A)
    gr = gains[perm].reshape(1, QK_A)
    blk = np.arange(QK_A) // HEAD_DIM
    avg = jnp.asarray((blk[:, None] == blk[None, :]).astype(np.float32) / HEAD_DIM, BF16)
    n_all = w_all.shape[1]

    def heads(n):
        return pl.BlockSpec((None, n, tm, LANES), lambda bi, i: (bi, 0, i, 0))

    def out(n):
        return jax.ShapeDtypeStruct((b, n, s, LANES), BF16)

    return pl.pallas_call(
        functools.partial(_inproj_even_kernel, qb_scale=q_scale),
        grid=(b, s // tm),
        in_specs=[
            pl.BlockSpec((None, tm, d), lambda bi, i: (bi, i, 0)),
            _const_spec((1, d)),
            _const_spec((d, n_all)),
            _const_spec((QK_A, QK_A)),
            _const_spec((1, QK_A)),
            _const_spec((1, QK_A)),
            pl.BlockSpec((tm, LANES), lambda bi, i: (i, 0)),
            pl.BlockSpec((tm, LANES), lambda bi, i: (i, 0)),
        ],
        out_specs=[heads(A_HEADS), heads(A_KV_HEADS), heads(A_KV_HEADS),
                   heads(B_HEADS), heads(B_KV_HEADS), heads(B_KV_HEADS)],
        out_shape=[out(A_HEADS), out(A_KV_HEADS), out(A_KV_HEADS),
                   out(B_HEADS), out(B_KV_HEADS), out(B_KV_HEADS)],
        compiler_params=_params("parallel", "parallel"),
        name="inproj_even",
    )(x, g_mix.reshape(1, d), w_all, avg, gq, gr, cos128, sin128)


def _inproj_latent_kernel(x_ref, g_ref, wd_ref, gq_ref, gkv_ref, wq_ref, wqr_ref, wk_ref, wv_ref,
                          cos_ref, sin_ref, q_ref, k_ref, v_ref, *, q_scale):
    h = _rms(x_ref[...], g_ref[...]).astype(BF16)
    c = jnp.dot(h, wd_ref[...], preferred_element_type=F32)
    cq = _rms(c[:, :Q_LORA], gq_ref[...]).astype(BF16)
    ckv = _rms(c[:, Q_LORA:Q_LORA + KV_LORA], gkv_ref[...]).astype(BF16)
    base = Q_LORA + KV_LORA
    cos = cos_ref[...]
    sin = sin_ref[...]
    k_rope = c[:, base:base + LANES] * cos + c[:, base + LANES:base + 2 * LANES] * sin
    yq = jnp.dot(cq, wq_ref[...], preferred_element_type=F32)
    yqr = jnp.dot(cq, wqr_ref[...], preferred_element_type=F32)
    yk = jnp.dot(ckv, wk_ref[...], preferred_element_type=F32)
    yv = jnp.dot(ckv, wv_ref[...], preferred_element_type=F32)
    for hd in range(C_HEADS):
        sl = slice(hd * LANES, (hd + 1) * LANES)
        q_ref[hd] = ((yq[:, sl] * cos + yqr[:, sl] * sin) * q_scale).astype(BF16)
        k_ref[hd] = (yk[:, sl] + k_rope).astype(BF16)
        v_ref[hd] = _with_ones_column(yv[:, sl]).astype(BF16)


def _pad_heads(w, n_heads, width, lo, hi, at):
    k = w.shape[0]
    w3 = w.reshape(k, n_heads, width)[:, :, lo:hi]
    w3 = jnp.pad(w3, ((0, 0), (0, 0), (at, LANES - at - (hi - lo))))
    return w3.reshape(k, n_heads * LANES)


def _inproj_latent(x, g_mix, w_down, q_gain, kv_gain, w_uq, w_ukv, cos128, sin128, *, tm):
    b, s, d = x.shape
    qk_dim = NOPE_DIM + ROPE_DIM
    q_scale = qk_dim ** -0.5 * LOG2E
    perm, sign = _rot_half_perm(ROPE_DIM, ROPE_DIM // 2)
    base = Q_LORA + KV_LORA
    w_kr = w_down[:, base:]
    pad = ((0, 0), (NOPE_DIM, LANES - NOPE_DIM - ROPE_DIM))
    wd = jnp.concatenate([w_down[:, :base], jnp.pad(w_kr, pad), jnp.pad(w_kr[:, perm] * sign, pad)],
                         axis=1).astype(BF16)
    wq = _pad_heads(w_uq, C_HEADS, qk_dim, 0, qk_dim, 0).astype(BF16)
    w_qr = w_uq.reshape(Q_LORA, C_HEADS, qk_dim)[:, :, NOPE_DIM:][:, :, perm] * sign
    wqr = _pad_heads(w_qr.reshape(Q_LORA, C_HEADS * ROPE_DIM), C_HEADS, ROPE_DIM, 0, ROPE_DIM,
                     NOPE_DIM).astype(BF16)
    wk = _pad_heads(w_ukv, C_HEADS, NOPE_DIM + V_DIM, 0, NOPE_DIM, 0).astype(BF16)
    wv = _pad_heads(w_ukv, C_HEADS, NOPE_DIM + V_DIM, NOPE_DIM, NOPE_DIM + V_DIM, 0).astype(BF16)
    nq = C_HEADS * LANES
    head_spec = pl.BlockSpec((None, C_HEADS, tm, LANES), lambda bi, i: (bi, 0, i, 0))
    out = jax.ShapeDtypeStruct((b, C_HEADS, s, LANES), BF16)
    return pl.pallas_call(
        functools.partial(_inproj_latent_kernel, q_scale=q_scale),
        grid=(b, s // tm),
        in_specs=[
            pl.BlockSpec((None, tm, d), lambda bi, i: (bi, i, 0)),
            _const_spec((1, d)),
            _const_spec(wd.shape),
            _const_spec((1, Q_LORA)),
            _const_spec((1, KV_LORA)),
            _const_spec((Q_LORA, nq)),
            _const_spec((Q_LORA, nq)),
            _const_spec((KV_LORA, nq)),
            _const_spec((KV_LORA, nq)),
            pl.BlockSpec((tm, LANES), lambda bi, i: (i, 0)),
            pl.BlockSpec((tm, LANES), lambda bi, i: (i, 0)),
        ],
        out_specs=[head_spec, head_spec, head_spec],
        out_shape=[out, out, out],
        compiler_params=_params("parallel", "parallel"),
        name="inproj_latent",
    )(x, g_mix.reshape(1, d), wd, q_gain.reshape(1, Q_LORA), kv_gain.reshape(1, KV_LORA),
      wq, wqr, wk, wv, cos128, sin128)


def _flash_rows(q, k_ref, v_ref, kv, acc_ref, *, tk):
    rows = q.shape[0]
    n_chunks = k_ref.shape[1] // tk
    acc_ref[...] = jnp.zeros_like(acc_ref)

    def body(j, m):
        start = pl.multiple_of(j * tk, tk)
        kc = k_ref[kv, pl.ds(start, tk), :]
        vc = v_ref[kv, pl.ds(start, tk), :]
        s = lax.dot_general(q, kc, (((1,), (1,)), ((), ())), preferred_element_type=F32)
        m_new = jnp.maximum(m, jnp.max(s, axis=1, keepdims=True))
        p = jnp.exp2(s - m_new).astype(BF16)
        alpha = jnp.exp2(m - m_new)
        acc_ref[...] = acc_ref[...] * alpha + jnp.dot(p, vc, preferred_element_type=F32)
        return m_new

    lax.fori_loop(0, n_chunks, body, jnp.full((rows, 1), -jnp.inf, F32))
    return acc_ref[...]


def _normalise(acc):
    return acc / acc[:, V_DIM:V_DIM + 1]


def _flash_kernel(q_ref, k_ref, v_ref, o_ref, acc_ref, *, tq, tk, shared_kv):
    if shared_kv:
        q = q_ref[...].reshape(2 * tq, LANES)
        o = _normalise(_flash_rows(q, k_ref, v_ref, 0, acc_ref, tk=tk))
        o0, o1 = o[:tq], o[tq:]
    else:
        o0 = _normalise(_flash_rows(q_ref[0], k_ref, v_ref, 0, acc_ref, tk=tk))
        o1 = _normalise(_flash_rows(q_ref[1], k_ref, v_ref, 1, acc_ref, tk=tk))
    lane = _lane_ids(o0.shape)
    o_ref[...] = jnp.where(lane < V_DIM, o0, _swap_halves(o1)).astype(o_ref.dtype)


def _flash_attention(q, k, v, *, tq, tk):
    b, h, s, _ = q.shape
    hkv = k.shape[1]
    shared_kv = h // hkv >= 2
    if shared_kv:
        group_pairs = h // hkv // 2
        kv_spec = pl.BlockSpec((None, 1, s, LANES), lambda bi, p, i: (bi, p // group_pairs, 0, 0),
                               pipeline_mode=pl.Buffered(1))
        rows = 2 * tq
    else:
        kv_spec = pl.BlockSpec((None, 2, s, LANES), lambda bi, p, i: (bi, p, 0, 0),
                               pipeline_mode=pl.Buffered(1))
        rows = tq
    return pl.pallas_call(
        functools.partial(_flash_kernel, tq=tq, tk=tk, shared_kv=shared_kv),
        grid=(b, h // 2, s // tq),
        in_specs=[
            pl.BlockSpec((None, 2, tq, LANES), lambda bi, p, i: (bi, p, i, 0)),
            kv_spec,
            kv_spec,
        ],
        out_specs=pl.BlockSpec((None, tq, LANES), lambda bi, p, i: (bi, i, p)),
        out_shape=jax.ShapeDtypeStruct((b, s, h * V_DIM), BF16),
        scratch_shapes=[pltpu.VMEM((rows, LANES), F32)],
        compiler_params=_params("parallel", "parallel", "arbitrary"),
        name="flash_shared" if shared_kv else "flash_mha",
    )(q, k, v)


def _window_kernel(sink_ref, q_ref, kp_ref, km_ref, kn_ref, vp_ref, vm_ref, vn_ref, o_ref,
                   *, tq, seq, group):
    g = pl.program_id(1)
    i = pl.program_id(2)
    kcat = jnp.concatenate([kp_ref[...], km_ref[...], kn_ref[...]], axis=0)
    vcat = jnp.concatenate([vp_ref[...], vm_ref[...], vn_ref[...]], axis=0)
    nk = tq + 2 * WINDOW
    q = q_ref[...].reshape(group * tq, LANES)
    s_all = lax.dot_general(q, kcat, (((1,), (1,)), ((), ())), preferred_element_type=F32)
    row = lax.broadcasted_iota(jnp.int32, (tq, nk), 0)
    col = lax.broadcasted_iota(jnp.int32, (tq, nk), 1)
    dist = jnp.abs(col - WINDOW - row)
    kpos = i * tq - WINDOW + col
    valid = (dist <= WINDOW) & (kpos >= 0) & (kpos < seq)
    dist_f = dist.astype(F32)
    outs = []
    for hh in range(group):
        head = g * group + hh
        slope = jnp.exp2(-8.0 * (head + 1).astype(F32) / B_HEADS) * LOG2E
        sink = sink_ref[head] * LOG2E
        sc = jnp.where(valid, s_all[hh * tq:(hh + 1) * tq] - slope * dist_f, -jnp.inf)
        m = jnp.maximum(jnp.max(sc, axis=1, keepdims=True), sink)
        e = jnp.exp2(sc - m).astype(BF16)
        acc = jnp.dot(e, vcat, preferred_element_type=F32)
        denom = acc[:, V_DIM:V_DIM + 1] + jnp.exp2(sink - m)
        outs.append(acc / denom)
    lane = _lane_ids(outs[0].shape)
    pairs = [jnp.where(lane < V_DIM, outs[2 * c], _swap_halves(outs[2 * c + 1]))
             for c in range(group // 2)]
    o_ref[...] = jnp.concatenate(pairs, axis=1).astype(o_ref.dtype)


def _window_attention(q, k, v, sink, *, tq):
    b, h, s, _ = q.shape
    hkv = k.shape[1]
    group = h // hkv
    per = tq // WINDOW
    last = s // WINDOW - 1

    def main(bi, g, i):
        return (bi, g, i, 0)

    def prev(bi, g, i):
        return (bi, g, jnp.maximum(i * per - 1, 0), 0)

    def nxt(bi, g, i):
        return (bi, g, jnp.minimum((i + 1) * per, last), 0)

    halo = lambda f: pl.BlockSpec((None, None, WINDOW, LANES), f)
    mid = pl.BlockSpec((None, None, tq, LANES), main)
    return pl.pallas_call(
        functools.partial(_window_kernel, tq=tq, seq=s, group=group),
        grid=(b, hkv, s // tq),
        in_specs=[
            pl.BlockSpec(memory_space=pltpu.SMEM),
            pl.BlockSpec((None, group, tq, LANES), main),
            halo(prev), mid, halo(nxt),
            halo(prev), mid, halo(nxt),
        ],
        out_specs=pl.BlockSpec((None, tq, group * V_DIM), lambda bi, g, i: (bi, i, g)),
        out_shape=jax.ShapeDtypeStruct((b, s, h * V_DIM), BF16),
        compiler_params=_params("parallel", "parallel", "parallel"),
        name="window_attention",
    )(sink, q, k, k, k, v, v, v)


def _outproj_kernel(x_ref, *refs):
    o_ref = refs[-1]
    acc = x_ref[...]
    n = (len(refs) - 1) // 2
    for a_ref, w_ref in zip(refs[:n], refs[n:2 * n]):
        acc = acc + jnp.dot(a_ref[...], w_ref[...], preferred_element_type=F32)
    o_ref[...] = acc


def _outproj(x, acts, weights, *, tm):
    b, s, d = x.shape
    row = lambda w: pl.BlockSpec((None, tm, w), lambda bi, i: (bi, i, 0))
    return pl.pallas_call(
        _outproj_kernel,
        grid=(b, s // tm),
        in_specs=[row(d)] + [row(a.shape[-1]) for a in acts] + [_const_spec(w.shape) for w in weights],
        out_specs=row(d),
        out_shape=jax.ShapeDtypeStruct((b, s, d), F32),
        compiler_params=_params("parallel", "parallel"),
        name="outproj",
    )(x, *acts, *weights)


FF_CHUNK = D_FF // 2
HALO = 8


def _ffn_kernel(xp_ref, x_ref, xn_ref, g_ref, wg_ref, wv_ref, cw_ref, cb_ref, wd_ref, gf_ref, o_ref,
                *, tm, final_norm):
    i = pl.program_id(1)
    n_blocks = pl.num_programs(1)
    x = x_ref[...]
    xp = jnp.where(i > 0, xp_ref[...], 0.0)
    xn = jnp.where(i < n_blocks - 1, xn_ref[...], 0.0)
    x_ext = jnp.concatenate([xp, x, xn], axis=0)
    h = _rms(x_ext, g_ref[...]).astype(BF16)
    ext = tm + 2 * HALO
    acc = x
    for c in range(D_FF // FF_CHUNK):
        sl = slice(c * FF_CHUNK, (c + 1) * FF_CHUNK)
        gate = jnp.dot(h, wg_ref[:, sl], preferred_element_type=F32)
        val = jnp.dot(h, wv_ref[:, sl], preferred_element_type=F32)[HALO:HALO + tm]
        g_prev = pltpu.roll(gate, 1, axis=0)[HALO:HALO + tm]
        g_next = pltpu.roll(gate, ext - 1, axis=0)[HALO:HALO + tm]
        gc = (cb_ref[:, sl] + g_prev * cw_ref[0:1, sl] + gate[HALO:HALO + tm] * cw_ref[1:2, sl]
              + g_next * cw_ref[2:3, sl])
        act = (gc * jax.nn.sigmoid(gc) * val).astype(BF16)
        acc = acc + jnp.dot(act, wd_ref[sl, :], preferred_element_type=F32)
    if final_norm:
        acc = _rms(acc, gf_ref[...])
    o_ref[...] = acc


def _ffn(x, g_ffn, w_up, conv_w, conv_b, w_down, g_final, *, tm, final_norm):
    b, s, d = x.shape
    per = tm // HALO
    last = s // HALO - 1
    wg = w_up[:, :D_FF].astype(BF16)
    wv = w_up[:, D_FF:].astype(BF16)
    wd = w_down.astype(BF16)
    halo = lambda f: pl.BlockSpec((None, HALO, d), f)
    return pl.pallas_call(
        functools.partial(_ffn_kernel, tm=tm, final_norm=final_norm),
        grid=(b, s // tm),
        in_specs=[
            halo(lambda bi, i: (bi, jnp.maximum(i * per - 1, 0), 0)),
            pl.BlockSpec((None, tm, d), lambda bi, i: (bi, i, 0)),
            halo(lambda bi, i: (bi, jnp.minimum((i + 1) * per, last), 0)),
            _const_spec((1, d)),
            _const_spec(wg.shape),
            _const_spec(wv.shape),
            _const_spec(conv_w.shape),
            _const_spec((1, D_FF)),
            _const_spec(wd.shape),
            _const_spec((1, d)),
        ],
        out_specs=pl.BlockSpec((None, tm, d), lambda bi, i: (bi, i, 0)),
        out_shape=jax.ShapeDtypeStruct((b, s, d), F32),
        compiler_params=_params("parallel", "parallel"),
        name="ffn_final" if final_norm else "ffn",
    )(x, x, x, g_ffn.reshape(1, d), wg, wv, conv_w, conv_b.reshape(1, D_FF), wd, g_final.reshape(1, d))


def _rope_angles(pos, dim):
    inv_freq = ROPE_THETA ** (-jnp.arange(0, dim, 2, dtype=F32) / dim)
    ang = pos.astype(F32)[:, None] * inv_freq[None, :]
    return jnp.cos(ang), jnp.sin(ang)


def _rope_tables(s):
    t = jnp.arange(s)
    cos_r, sin_r = _rope_angles(t // GRID_W, HEAD_DIM // 2)
    cos_c, sin_c = _rope_angles(t % GRID_W, HEAD_DIM // 2)
    cos_ax = jnp.tile(jnp.concatenate([cos_r, cos_r, cos_c, cos_c], axis=1), (1, 2))
    sin_ax = jnp.tile(jnp.concatenate([sin_r, sin_r, sin_c, sin_c], axis=1), (1, 2))
    cos_t, sin_t = _rope_angles(t, ROPE_DIM)
    tail = jnp.zeros((s, LANES - NOPE_DIM - ROPE_DIM), F32)
    cos_l = jnp.concatenate([jnp.ones((s, NOPE_DIM), F32), cos_t, cos_t, tail], axis=1)
    sin_l = jnp.concatenate([jnp.zeros((s, NOPE_DIM), F32), sin_t, sin_t, tail], axis=1)
    return cos_ax, sin_ax, cos_l, sin_l


TM = 512
TQ_SHARED = 512
TQ_MHA = 1024
TK = 512
TQ_WINDOW = 256


def _trunk(x, p, tables):
    cos_ax, sin_ax, cos_l, sin_l = tables
    s = x.shape[1]
    tm = min(TM, s)
    qa, ka, va, qb, kb, vb = _inproj_even(x, p["norm_mix"][0], p["e_w_in"][0], p["e_q_gain"][0],
                                          p["e_k_gain"][0], cos_ax, sin_ax, tm=tm)
    oa = _flash_attention(qa, ka, va, tq=min(TQ_SHARED, s), tk=min(TK, s))
    ob = _window_attention(qb, kb, vb, p["e_sink"][0], tq=min(TQ_WINDOW, s))
    w_out = p["e_w_out"][0].astype(BF16)
    n_a = A_HEADS * HEAD_DIM
    x = _outproj(x, [oa, ob], [w_out[:n_a], w_out[n_a:]], tm=tm)
    x = _ffn(x, p["norm_ffn"][0], p["f_w_up"][0], p["f_conv_w"][0], p["f_conv_b"][0], p["f_w_down"][0],
             p["norm_final"], tm=tm, final_norm=False)
    q, k, v = _inproj_latent(x, p["norm_mix"][1], p["o_w_down"][0], p["o_q_gain"][0], p["o_kv_gain"][0],
                             p["o_w_uq"][0], p["o_w_ukv"][0], cos_l, sin_l, tm=tm)
    o = _flash_attention(q, k, v, tq=min(TQ_MHA, s), tk=min(TK, s))
    x = _outproj(x, [o], [p["o_w_out"][0].astype(BF16)], tm=tm)
    x = _ffn(x, p["norm_ffn"][1], p["f_w_up"][1], p["f_conv_w"][1], p["f_conv_b"][1], p["f_w_down"][1],
             p["norm_final"], tm=tm, final_norm=True)
    return x


def kernel(x_prompt, x_sample, norm_mix, norm_ffn, norm_final, e_w_in, e_q_gain, e_k_gain, e_sink, e_w_out, o_w_down, o_q_gain, o_kv_gain, o_w_uq, o_w_ukv, o_w_out, f_w_up, f_conv_w, f_conv_b, f_w_down):
    p = dict(norm_mix=norm_mix, norm_ffn=norm_ffn, norm_final=norm_final, e_w_in=e_w_in, e_q_gain=e_q_gain,
             e_k_gain=e_k_gain, e_sink=e_sink, e_w_out=e_w_out, o_w_down=o_w_down, o_q_gain=o_q_gain,
             o_kv_gain=o_kv_gain, o_w_uq=o_w_uq, o_w_ukv=o_w_ukv, o_w_out=o_w_out, f_w_up=f_w_up,
             f_conv_w=f_conv_w, f_conv_b=f_conv_b, f_w_down=f_w_down)
    tables = _rope_tables(x_prompt.shape[1])
    return _trunk(x_prompt, p, tables), _trunk(x_sample, p, tables)
```

```python
import functools
import math

import numpy as np
import jax
import jax.numpy as jnp
from jax import lax
from jax.experimental import pallas as pl
from jax.experimental.pallas import tpu as pltpu

D_MODEL = 1024
GRID_W = 64
HEAD_DIM = 64
A_HEADS = 8
A_KV_HEADS = 2
B_HEADS = 8
B_KV_HEADS = 2
WINDOW = 128
C_HEADS = 16
Q_LORA = 384
KV_LORA = 256
NOPE_DIM = 64
ROPE_DIM = 32
V_DIM = 64
D_FF = 2816
CONV_W = 3
ROPE_THETA = 10000.0
EPS = 1e-6

LANES = 128
LOG2E = math.log2(math.e)
VMEM_LIMIT = 56 * 1024 * 1024

F32 = jnp.float32
BF16 = jnp.bfloat16

QK_A = (A_HEADS + A_KV_HEADS) * HEAD_DIM


def _params(*sem):
    return pltpu.CompilerParams(dimension_semantics=sem, vmem_limit_bytes=VMEM_LIMIT)


def _const_spec(shape):
    nd = len(shape)
    return pl.BlockSpec(shape, lambda *_: (0,) * nd, pipeline_mode=pl.Buffered(1))


def _rms(x, gain):
    ms = jnp.mean(x * x, axis=-1, keepdims=True)
    return x * lax.rsqrt(ms + EPS) * gain


def _lane_ids(shape):
    return lax.broadcasted_iota(jnp.int32, shape, len(shape) - 1)


def _low_half(x):
    return jnp.where(_lane_ids(x.shape) < HEAD_DIM, x, 0.0)


def _swap_halves(x):
    return pltpu.roll(x, HEAD_DIM, axis=1)


def _with_ones_column(v):
    lane = _lane_ids(v.shape)
    return jnp.where(lane < V_DIM, v, jnp.where(lane == V_DIM, 1.0, 0.0))


def _inproj_even_kernel(x_ref, g_ref, w_ref, avg_ref, gq_ref, gr_ref, cos_ref, sin_ref,
                        qa_ref, ka_ref, va_ref, qb_ref, kb_ref, vb_ref, *, qb_scale):
    h = _rms(x_ref[...], g_ref[...]).astype(BF16)
    y = jnp.dot(h, w_ref[...], preferred_element_type=F32)
    yqk = y[:, :QK_A]
    yrot = y[:, QK_A:2 * QK_A]
    sq = yqk * yqk
    hi = sq.astype(BF16)
    lo = (sq - hi.astype(F32)).astype(BF16)
    msq = (jnp.dot(hi, avg_ref[...], preferred_element_type=F32)
           + jnp.dot(lo, avg_ref[...], preferred_element_type=F32))
    r = lax.rsqrt(msq + EPS)
    cos = cos_ref[...]
    sin = sin_ref[...]
    for c in range(QK_A // LANES):
        sl = slice(c * LANES, (c + 1) * LANES)
        a = (yqk[:, sl] * (gq_ref[:, sl] * cos) + yrot[:, sl] * (gr_ref[:, sl] * sin)) * r[:, sl]
        even = _low_half(a).astype(BF16)
        odd = _low_half(_swap_halves(a)).astype(BF16)
        if c < A_HEADS // 2:
            qa_ref[2 * c] = even
            qa_ref[2 * c + 1] = odd
        else:
            ka_ref[0] = even
            ka_ref[1] = odd
    base = 2 * QK_A
    v = y[:, base:base + LANES]
    va_ref[0] = _with_ones_column(v).astype(BF16)
    va_ref[1] = _with_ones_column(_swap_halves(v)).astype(BF16)
    base += LANES
    for c in range(B_HEADS // 2):
        a = y[:, base + c * LANES: base + (c + 1) * LANES] * qb_scale
        qb_ref[2 * c] = _low_half(a).astype(BF16)
        qb_ref[2 * c + 1] = _low_half(_swap_halves(a)).astype(BF16)
    base += B_HEADS * HEAD_DIM
    kk = y[:, base:base + LANES]
    kb_ref[0] = _low_half(kk).astype(BF16)
    kb_ref[1] = _low_half(_swap_halves(kk)).astype(BF16)
    base += LANES
    v = y[:, base:base + LANES]
    vb_ref[0] = _with_ones_column(v).astype(BF16)
    vb_ref[1] = _with_ones_column(_swap_halves(v)).astype(BF16)


def _rot_half_perm(n_cols, half):
    col = np.arange(n_cols)
    first = (col % (2 * half)) < half
    perm = np.where(first, col + half, col - half)
    sign = np.where(first, -1.0, 1.0).astype(np.float32)
    return perm, sign


def _inproj_even(x, g_mix, w_in, q_gain, k_gain, cos128, sin128, *, tm):
    b, s, d = x.shape
    q_scale = HEAD_DIM ** -0.5 * LOG2E
    w_qk = w_in[:, :QK_A]
    perm, sign = _rot_half_perm(QK_A, HEAD_DIM // 4)
    w_rot = w_qk[:, perm] * sign
    w_all = jnp.concatenate([w_qk, w_rot, w_in[:, QK_A:]], axis=1).astype(BF16)
    gains = jnp.concatenate([jnp.tile(q_gain * q_scale, A_HEADS), jnp.tile(k_gain, A_KV_HEADS)])
    gq = gains.reshape(1, QK_A)
    gr = gains[perm].reshape(1, QK_A)
    blk = np.arange(QK_A) // HEAD_DIM
    avg = jnp.asarray((blk[:, None] == blk[None, :]).astype(np.float32) / HEAD_DIM, BF16)
    n_all = w_all.shape[1]

    def heads(n):
        return pl.BlockSpec((None, n, tm, LANES), lambda bi, i: (bi, 0, i, 0))

    def out(n):
        return jax.ShapeDtypeStruct((b, n, s, LANES), BF16)

    return pl.pallas_call(
        functools.partial(_inproj_even_kernel, qb_scale=q_scale),
        grid=(b, s // tm),
        in_specs=[
            pl.BlockSpec((None, tm, d), lambda bi, i: (bi, i, 0)),
            _const_spec((1, d)),
            _const_spec((d, n_all)),
            _const_spec((QK_A, QK_A)),
            _const_spec((1, QK_A)),
            _const_spec((1, QK_A)),
            pl.BlockSpec((tm, LANES), lambda bi, i: (i, 0)),
            pl.BlockSpec((tm, LANES), lambda bi, i: (i, 0)),
        ],
        out_specs=[heads(A_HEADS), heads(A_KV_HEADS), heads(A_KV_HEADS),
                   heads(B_HEADS), heads(B_KV_HEADS), heads(B_KV_HEADS)],
        out_shape=[out(A_HEADS), out(A_KV_HEADS), out(A_KV_HEADS),
                   out(B_HEADS), out(B_KV_HEADS), out(B_KV_HEADS)],
        compiler_params=_params("parallel", "parallel"),
        name="inproj_even",
    )(x, g_mix.reshape(1, d), w_all, avg, gq, gr, cos128, sin128)


def _inproj_latent_kernel(x_ref, g_ref, wd_ref, gq_ref, gkv_ref, wq_ref, wqr_ref, wk_ref, wv_ref,
                          cos_ref, sin_ref, q_ref, k_ref, v_ref, *, q_scale):
    h = _rms(x_ref[...], g_ref[...]).astype(BF16)
    c = jnp.dot(h, wd_ref[...], preferred_element_type=F32)
    cq = _rms(c[:, :Q_LORA], gq_ref[...]).astype(BF16)
    ckv = _rms(c[:, Q_LORA:Q_LORA + KV_LORA], gkv_ref[...]).astype(BF16)
    base = Q_LORA + KV_LORA
    cos = cos_ref[...]
    sin = sin_ref[...]
    k_rope = c[:, base:base + LANES] * cos + c[:, base + LANES:base + 2 * LANES] * sin
    yq = jnp.dot(cq, wq_ref[...], preferred_element_type=F32)
    yqr = jnp.dot(cq, wqr_ref[...], preferred_element_type=F32)
    yk = jnp.dot(ckv, wk_ref[...], preferred_element_type=F32)
    yv = jnp.dot(ckv, wv_ref[...], preferred_element_type=F32)
    for hd in range(C_HEADS):
        sl = slice(hd * LANES, (hd + 1) * LANES)
        q_ref[hd] = ((yq[:, sl] * cos + yqr[:, sl] * sin) * q_scale).astype(BF16)
        k_ref[hd] = (yk[:, sl] + k_rope).astype(BF16)
        v_ref[hd] = _with_ones_column(yv[:, sl]).astype(BF16)


def _pad_heads(w, n_heads, width, lo, hi, at):
    k = w.shape[0]
    w3 = w.reshape(k, n_heads, width)[:, :, lo:hi]
    w3 = jnp.pad(w3, ((0, 0), (0, 0), (at, LANES - at - (hi - lo))))
    return w3.reshape(k, n_heads * LANES)


def _inproj_latent(x, g_mix, w_down, q_gain, kv_gain, w_uq, w_ukv, cos128, sin128, *, tm):
    b, s, d = x.shape
    qk_dim = NOPE_DIM + ROPE_DIM
    q_scale = qk_dim ** -0.5 * LOG2E
    perm, sign = _rot_half_perm(ROPE_DIM, ROPE_DIM // 2)
    base = Q_LORA + KV_LORA
    w_kr = w_down[:, base:]
    pad = ((0, 0), (NOPE_DIM, LANES - NOPE_DIM - ROPE_DIM))
    wd = jnp.concatenate([w_down[:, :base], jnp.pad(w_kr, pad), jnp.pad(w_kr[:, perm] * sign, pad)],
                         axis=1).astype(BF16)
    wq = _pad_heads(w_uq, C_HEADS, qk_dim, 0, qk_dim, 0).astype(BF16)
    w_qr = w_uq.reshape(Q_LORA, C_HEADS, qk_dim)[:, :, NOPE_DIM:][:, :, perm] * sign
    wqr = _pad_heads(w_qr.reshape(Q_LORA, C_HEADS * ROPE_DIM), C_HEADS, ROPE_DIM, 0, ROPE_DIM,
                     NOPE_DIM).astype(BF16)
    wk = _pad_heads(w_ukv, C_HEADS, NOPE_DIM + V_DIM, 0, NOPE_DIM, 0).astype(BF16)
    wv = _pad_heads(w_ukv, C_HEADS, NOPE_DIM + V_DIM, NOPE_DIM, NOPE_DIM + V_DIM, 0).astype(BF16)
    nq = C_HEADS * LANES
    head_spec = pl.BlockSpec((None, C_HEADS, tm, LANES), lambda bi, i: (bi, 0, i, 0))
    out = jax.ShapeDtypeStruct((b, C_HEADS, s, LANES), BF16)
    return pl.pallas_call(
        functools.partial(_inproj_latent_kernel, q_scale=q_scale),
        grid=(b, s // tm),
        in_specs=[
            pl.BlockSpec((None, tm, d), lambda bi, i: (bi, i, 0)),
            _const_spec((1, d)),
            _const_spec(wd.shape),
            _const_spec((1, Q_LORA)),
            _const_spec((1, KV_LORA)),
            _const_spec((Q_LORA, nq)),
            _const_spec((Q_LORA, nq)),
            _const_spec((KV_LORA, nq)),
            _const_spec((KV_LORA, nq)),
            pl.BlockSpec((tm, LANES), lambda bi, i: (i, 0)),
            pl.BlockSpec((tm, LANES), lambda bi, i: (i, 0)),
        ],
        out_specs=[head_spec, head_spec, head_spec],
        out_shape=[out, out, out],
        compiler_params=_params("parallel", "parallel"),
        name="inproj_latent",
    )(x, g_mix.reshape(1, d), wd, q_gain.reshape(1, Q_LORA), kv_gain.reshape(1, KV_LORA),
      wq, wqr, wk, wv, cos128, sin128)


SCORE_RING = 4
SCORE_AHEAD = 2


def _flash_rows(q, k_ref, v_ref, kv, s_ref, acc_ref, *, tk):
    rows = q.shape[0]
    n_chunks = k_ref.shape[1] // tk
    ring = s_ref.shape[0]
    assert n_chunks % ring == 0 and SCORE_AHEAD < ring
    acc_ref[...] = jnp.zeros_like(acc_ref)

    def scores(j, slot):
        start = pl.multiple_of(j * tk, tk)
        kc = k_ref[kv, pl.ds(start, tk), :]
        s_ref[slot] = lax.dot_general(q, kc, (((1,), (1,)), ((), ())), preferred_element_type=F32)

    def softmax_pv(j, slot, m):
        start = pl.multiple_of(j * tk, tk)
        vc = v_ref[kv, pl.ds(start, tk), :]
        s = s_ref[slot]
        m_new = jnp.maximum(m, jnp.max(s, axis=1, keepdims=True))
        p = jnp.exp2(s - m_new).astype(BF16)
        alpha = jnp.exp2(m - m_new)
        acc_ref[...] = acc_ref[...] * alpha + jnp.dot(p, vc, preferred_element_type=F32)
        return m_new

    def body(i, m):
        for r in range(ring):
            scores(i * ring + r + SCORE_AHEAD, (r + SCORE_AHEAD) % ring)
            m = softmax_pv(i * ring + r, r, m)
        return m

    for j in range(SCORE_AHEAD):
        scores(j, j)
    m = lax.fori_loop(0, n_chunks // ring - 1, body, jnp.full((rows, 1), -jnp.inf, F32))
    for r in range(ring):
        j = n_chunks - ring + r
        if j + SCORE_AHEAD < n_chunks:
            scores(j + SCORE_AHEAD, (r + SCORE_AHEAD) % ring)
        m = softmax_pv(j, r, m)
    return acc_ref[...]


def _normalise(acc):
    return acc / acc[:, V_DIM:V_DIM + 1]


def _flash_kernel(q_ref, k_ref, v_ref, o_ref, s_ref, acc_ref, *, tq, tk, shared_kv):
    if shared_kv:
        q = q_ref[...].reshape(2 * tq, LANES)
        o = _normalise(_flash_rows(q, k_ref, v_ref, 0, s_ref, acc_ref, tk=tk))
        o0, o1 = o[:tq], o[tq:]
    else:
        o0 = _normalise(_flash_rows(q_ref[0], k_ref, v_ref, 0, s_ref, acc_ref, tk=tk))
        o1 = _normalise(_flash_rows(q_ref[1], k_ref, v_ref, 1, s_ref, acc_ref, tk=tk))
    lane = _lane_ids(o0.shape)
    o_ref[...] = jnp.where(lane < V_DIM, o0, _swap_halves(o1)).astype(o_ref.dtype)


def _flash_attention(q, k, v, *, tq, tk):
    b, h, s, _ = q.shape
    hkv = k.shape[1]
    shared_kv = h // hkv >= 2
    if shared_kv:
        group_pairs = h // hkv // 2
        kv_spec = pl.BlockSpec((None, 1, s, LANES), lambda bi, p, i: (bi, p // group_pairs, 0, 0),
                               pipeline_mode=pl.Buffered(1))
        rows = 2 * tq
    else:
        kv_spec = pl.BlockSpec((None, 2, s, LANES), lambda bi, p, i: (bi, p, 0, 0),
                               pipeline_mode=pl.Buffered(1))
        rows = tq
    return pl.pallas_call(
        functools.partial(_flash_kernel, tq=tq, tk=tk, shared_kv=shared_kv),
        grid=(b, h // 2, s // tq),
        in_specs=[
            pl.BlockSpec((None, 2, tq, LANES), lambda bi, p, i: (bi, p, i, 0)),
            kv_spec,
            kv_spec,
        ],
        out_specs=pl.BlockSpec((None, tq, LANES), lambda bi, p, i: (bi, i, p)),
        out_shape=jax.ShapeDtypeStruct((b, s, h * V_DIM), BF16),
        scratch_shapes=[pltpu.VMEM((SCORE_RING, rows, tk), F32), pltpu.VMEM((rows, LANES), F32)],
        compiler_params=_params("parallel", "parallel", "arbitrary"),
        name="flash_shared" if shared_kv else "flash_mha",
    )(q, k, v)


def _window_kernel(sink_ref, q_ref, kp_ref, km_ref, kn_ref, vp_ref, vm_ref, vn_ref, o_ref,
                   *, tq, seq, group):
    g = pl.program_id(1)
    i = pl.program_id(2)
    kcat = jnp.concatenate([kp_ref[...], km_ref[...], kn_ref[...]], axis=0)
    vcat = jnp.concatenate([vp_ref[...], vm_ref[...], vn_ref[...]], axis=0)
    nk = tq + 2 * WINDOW
    q = q_ref[...].reshape(group * tq, LANES)
    s_all = lax.dot_general(q, kcat, (((1,), (1,)), ((), ())), preferred_element_type=F32)
    row = lax.broadcasted_iota(jnp.int32, (tq, nk), 0)
    col = lax.broadcasted_iota(jnp.int32, (tq, nk), 1)
    dist = jnp.abs(col - WINDOW - row)
    kpos = i * tq - WINDOW + col
    valid = (dist <= WINDOW) & (kpos >= 0) & (kpos < seq)
    dist_f = dist.astype(F32)
    outs = []
    for hh in range(group):
        head = g * group + hh
        slope = jnp.exp2(-8.0 * (head + 1).astype(F32) / B_HEADS) * LOG2E
        sink = sink_ref[head] * LOG2E
        sc = jnp.where(valid, s_all[hh * tq:(hh + 1) * tq] - slope * dist_f, -jnp.inf)
        m = jnp.maximum(jnp.max(sc, axis=1, keepdims=True), sink)
        e = jnp.exp2(sc - m).astype(BF16)
        acc = jnp.dot(e, vcat, preferred_element_type=F32)
        denom = acc[:, V_DIM:V_DIM + 1] + jnp.exp2(sink - m)
        outs.append(acc / denom)
    lane = _lane_ids(outs[0].shape)
    pairs = [jnp.where(lane < V_DIM, outs[2 * c], _swap_halves(outs[2 * c + 1]))
             for c in range(group // 2)]
    o_ref[...] = jnp.concatenate(pairs, axis=1).astype(o_ref.dtype)


def _window_attention(q, k, v, sink, *, tq):
    b, h, s, _ = q.shape
    hkv = k.shape[1]
    group = h // hkv
    per = tq // WINDOW
    last = s // WINDOW - 1

    def main(bi, g, i):
        return (bi, g, i, 0)

    def prev(bi, g, i):
        return (bi, g, jnp.maximum(i * per - 1, 0), 0)

    def nxt(bi, g, i):
        return (bi, g, jnp.minimum((i + 1) * per, last), 0)

    halo = lambda f: pl.BlockSpec((None, None, WINDOW, LANES), f)
    mid = pl.BlockSpec((None, None, tq, LANES), main)
    return pl.pallas_call(
        functools.partial(_window_kernel, tq=tq, seq=s, group=group),
        grid=(b, hkv, s // tq),
        in_specs=[
            pl.BlockSpec(memory_space=pltpu.SMEM),
            pl.BlockSpec((None, group, tq, LANES), main),
            halo(prev), mid, halo(nxt),
            halo(prev), mid, halo(nxt),
        ],
        out_specs=pl.BlockSpec((None, tq, group * V_DIM), lambda bi, g, i: (bi, i, g)),
        out_shape=jax.ShapeDtypeStruct((b, s, h * V_DIM), BF16),
        compiler_params=_params("parallel", "parallel", "parallel"),
        name="window_attention",
    )(sink, q, k, k, k, v, v, v)


def _outproj_kernel(x_ref, *refs):
    o_ref = refs[-1]
    acc = x_ref[...]
    n = (len(refs) - 1) // 2
    for a_ref, w_ref in zip(refs[:n], refs[n:2 * n]):
        acc = acc + jnp.dot(a_ref[...], w_ref[...], preferred_element_type=F32)
    o_ref[...] = acc


def _outproj(x, acts, weights, *, tm):
    b, s, d = x.shape
    row = lambda w: pl.BlockSpec((None, tm, w), lambda bi, i: (bi, i, 0))
    return pl.pallas_call(
        _outproj_kernel,
        grid=(b, s // tm),
        in_specs=[row(d)] + [row(a.shape[-1]) for a in acts] + [_const_spec(w.shape) for w in weights],
        out_specs=row(d),
        out_shape=jax.ShapeDtypeStruct((b, s, d), F32),
        compiler_params=_params("parallel", "parallel"),
        name="outproj",
    )(x, *acts, *weights)


FF_CHUNK = D_FF // 2
HALO = 8


def _ffn_kernel(xp_ref, x_ref, xn_ref, g_ref, wg_ref, wv_ref, cw_ref, cb_ref, wd_ref, gf_ref, o_ref,
                *, tm, final_norm):
    i = pl.program_id(1)
    n_blocks = pl.num_programs(1)
    x = x_ref[...]
    xp = jnp.where(i > 0, xp_ref[...], 0.0)
    xn = jnp.where(i < n_blocks - 1, xn_ref[...], 0.0)
    x_ext = jnp.concatenate([xp, x, xn], axis=0)
    h = _rms(x_ext, g_ref[...]).astype(BF16)
    ext = tm + 2 * HALO
    acc = x
    for c in range(D_FF // FF_CHUNK):
        sl = slice(c * FF_CHUNK, (c + 1) * FF_CHUNK)
        gate = jnp.dot(h, wg_ref[:, sl], preferred_element_type=F32)
        val = jnp.dot(h, wv_ref[:, sl], preferred_element_type=F32)[HALO:HALO + tm]
        g_prev = pltpu.roll(gate, 1, axis=0)[HALO:HALO + tm]
        g_next = pltpu.roll(gate, ext - 1, axis=0)[HALO:HALO + tm]
        gc = (cb_ref[:, sl] + g_prev * cw_ref[0:1, sl] + gate[HALO:HALO + tm] * cw_ref[1:2, sl]
              + g_next * cw_ref[2:3, sl])
        act = (gc * jax.nn.sigmoid(gc) * val).astype(BF16)
        acc = acc + jnp.dot(act, wd_ref[sl, :], preferred_element_type=F32)
    if final_norm:
        acc = _rms(acc, gf_ref[...])
    o_ref[...] = acc


def _ffn(x, g_ffn, w_up, conv_w, conv_b, w_down, g_final, *, tm, final_norm):
    b, s, d = x.shape
    per = tm // HALO
    last = s // HALO - 1
    wg = w_up[:, :D_FF].astype(BF16)
    wv = w_up[:, D_FF:].astype(BF16)
    wd = w_down.astype(BF16)
    halo = lambda f: pl.BlockSpec((None, HALO, d), f)
    return pl.pallas_call(
        functools.partial(_ffn_kernel, tm=tm, final_norm=final_norm),
        grid=(b, s // tm),
        in_specs=[
            halo(lambda bi, i: (bi, jnp.maximum(i * per - 1, 0), 0)),
            pl.BlockSpec((None, tm, d), lambda bi, i: (bi, i, 0)),
            halo(lambda bi, i: (bi, jnp.minimum((i + 1) * per, last), 0)),
            _const_spec((1, d)),
            _const_spec(wg.shape),
            _const_spec(wv.shape),
            _const_spec(conv_w.shape),
            _const_spec((1, D_FF)),
            _const_spec(wd.shape),
            _const_spec((1, d)),
        ],
        out_specs=pl.BlockSpec((None, tm, d), lambda bi, i: (bi, i, 0)),
        out_shape=jax.ShapeDtypeStruct((b, s, d), F32),
        compiler_params=_params("parallel", "parallel"),
        name="ffn_final" if final_norm else "ffn",
    )(x, x, x, g_ffn.reshape(1, d), wg, wv, conv_w, conv_b.reshape(1, D_FF), wd, g_final.reshape(1, d))


def _rope_angles(pos, dim):
    inv_freq = ROPE_THETA ** (-jnp.arange(0, dim, 2, dtype=F32) / dim)
    ang = pos.astype(F32)[:, None] * inv_freq[None, :]
    return jnp.cos(ang), jnp.sin(ang)


def _rope_tables(s):
    t = jnp.arange(s)
    cos_r, sin_r = _rope_angles(t // GRID_W, HEAD_DIM // 2)
    cos_c, sin_c = _rope_angles(t % GRID_W, HEAD_DIM // 2)
    cos_ax = jnp.tile(jnp.concatenate([cos_r, cos_r, cos_c, cos_c], axis=1), (1, 2))
    sin_ax = jnp.tile(jnp.concatenate([sin_r, sin_r, sin_c, sin_c], axis=1), (1, 2))
    cos_t, sin_t = _rope_angles(t, ROPE_DIM)
    tail = jnp.zeros((s, LANES - NOPE_DIM - ROPE_DIM), F32)
    cos_l = jnp.concatenate([jnp.ones((s, NOPE_DIM), F32), cos_t, cos_t, tail], axis=1)
    sin_l = jnp.concatenate([jnp.zeros((s, NOPE_DIM), F32), sin_t, sin_t, tail], axis=1)
    return cos_ax, sin_ax, cos_l, sin_l


TM = 512
TQ_SHARED = 512
TQ_MHA = 1024
TK = 512
TQ_WINDOW = 256


def _trunk(x, p, tables):
    cos_ax, sin_ax, cos_l, sin_l = tables
    s = x.shape[1]
    tm = min(TM, s)
    qa, ka, va, qb, kb, vb = _inproj_even(x, p["norm_mix"][0], p["e_w_in"][0], p["e_q_gain"][0],
                                          p["e_k_gain"][0], cos_ax, sin_ax, tm=tm)
    oa = _flash_attention(qa, ka, va, tq=min(TQ_SHARED, s), tk=min(TK, s // (2 * SCORE_RING)))
    ob = _window_attention(qb, kb, vb, p["e_sink"][0], tq=min(TQ_WINDOW, s))
    w_out = p["e_w_out"][0].astype(BF16)
    n_a = A_HEADS * HEAD_DIM
    x = _outproj(x, [oa, ob], [w_out[:n_a], w_out[n_a:]], tm=tm)
    x = _ffn(x, p["norm_ffn"][0], p["f_w_up"][0], p["f_conv_w"][0], p["f_conv_b"][0], p["f_w_down"][0],
             p["norm_final"], tm=tm, final_norm=False)
    q, k, v = _inproj_latent(x, p["norm_mix"][1], p["o_w_down"][0], p["o_q_gain"][0], p["o_kv_gain"][0],
                             p["o_w_uq"][0], p["o_w_ukv"][0], cos_l, sin_l, tm=tm)
    o = _flash_attention(q, k, v, tq=min(TQ_MHA, s), tk=min(TK, s // (2 * SCORE_RING)))
    x = _outproj(x, [o], [p["o_w_out"][0].astype(BF16)], tm=tm)
    x = _ffn(x, p["norm_ffn"][1], p["f_w_up"][1], p["f_conv_w"][1], p["f_conv_b"][1], p["f_w_down"][1],
             p["norm_final"], tm=tm, final_norm=True)
    return x


def kernel(x_prompt, x_sample, norm_mix, norm_ffn, norm_final, e_w_in, e_q_gain, e_k_gain, e_sink, e_w_out, o_w_down, o_q_gain, o_kv_gain, o_w_uq, o_w_ukv, o_w_out, f_w_up, f_conv_w, f_conv_b, f_w_down):
    p = dict(norm_mix=norm_mix, norm_ffn=norm_ffn, norm_final=norm_final, e_w_in=e_w_in, e_q_gain=e_q_gain,
             e_k_gain=e_k_gain, e_sink=e_sink, e_w_out=e_w_out, o_w_down=o_w_down, o_q_gain=o_q_gain,
             o_kv_gain=o_kv_gain, o_w_uq=o_w_uq, o_w_ukv=o_w_ukv, o_w_out=o_w_out, f_w_up=f_w_up,
             f_conv_w=f_conv_w, f_conv_b=f_conv_b, f_w_down=f_w_down)
    tables = _rope_tables(x_prompt.shape[1])
    return _trunk(x_prompt, p, tables), _trunk(x_sample, p, tables)
```

```python
import functools
import math

import numpy as np
import jax
import jax.numpy as jnp
from jax import lax
from jax.experimental import pallas as pl
from jax.experimental.pallas import tpu as pltpu

D_MODEL = 1024
GRID_W = 64
HEAD_DIM = 64
A_HEADS = 8
A_KV_HEADS = 2
B_HEADS = 8
B_KV_HEADS = 2
WINDOW = 128
C_HEADS = 16
Q_LORA = 384
KV_LORA = 256
NOPE_DIM = 64
ROPE_DIM = 32
V_DIM = 64
D_FF = 2816
CONV_W = 3
ROPE_THETA = 10000.0
EPS = 1e-6

LANES = 128
LOG2E = math.log2(math.e)
VMEM_LIMIT = 56 * 1024 * 1024

F32 = jnp.float32
BF16 = jnp.bfloat16

QK_A = (A_HEADS + A_KV_HEADS) * HEAD_DIM


def _params(*sem):
    return pltpu.CompilerParams(dimension_semantics=sem, vmem_limit_bytes=VMEM_LIMIT)


def _const_spec(shape):
    nd = len(shape)
    return pl.BlockSpec(shape, lambda *_: (0,) * nd, pipeline_mode=pl.Buffered(1))


def _rms(x, gain):
    ms = jnp.mean(x * x, axis=-1, keepdims=True)
    return x * lax.rsqrt(ms + EPS) * gain


def _lane_ids(shape):
    return lax.broadcasted_iota(jnp.int32, shape, len(shape) - 1)


def _low_half(x):
    return jnp.where(_lane_ids(x.shape) < HEAD_DIM, x, 0.0)


def _swap_halves(x):
    return pltpu.roll(x, HEAD_DIM, axis=1)


def _with_ones_column(v):
    lane = _lane_ids(v.shape)
    return jnp.where(lane < V_DIM, v, jnp.where(lane == V_DIM, 1.0, 0.0))


def _inproj_even_kernel(x_ref, g_ref, w_ref, avg_ref, gq_ref, gr_ref, cos_ref, sin_ref,
                        qa_ref, ka_ref, va_ref, qb_ref, kb_ref, vb_ref, *, qb_scale):
    h = _rms(x_ref[...], g_ref[...]).astype(BF16)
    y = jnp.dot(h, w_ref[...], preferred_element_type=F32)
    yqk = y[:, :QK_A]
    yrot = y[:, QK_A:2 * QK_A]
    sq = yqk * yqk
    hi = sq.astype(BF16)
    lo = (sq - hi.astype(F32)).astype(BF16)
    msq = (jnp.dot(hi, avg_ref[...], preferred_element_type=F32)
           + jnp.dot(lo, avg_ref[...], preferred_element_type=F32))
    r = lax.rsqrt(msq + EPS)
    cos = cos_ref[...]
    sin = sin_ref[...]
    for c in range(QK_A // LANES):
        sl = slice(c * LANES, (c + 1) * LANES)
        a = (yqk[:, sl] * (gq_ref[:, sl] * cos) + yrot[:, sl] * (gr_ref[:, sl] * sin)) * r[:, sl]
        even = _low_half(a).astype(BF16)
        odd = _low_half(_swap_halves(a)).astype(BF16)
        if c < A_HEADS // 2:
            qa_ref[2 * c] = even
            qa_ref[2 * c + 1] = odd
        else:
            ka_ref[0] = even
            ka_ref[1] = odd
    base = 2 * QK_A
    v = y[:, base:base + LANES]
    va_ref[0] = _with_ones_column(v).astype(BF16)
    va_ref[1] = _with_ones_column(_swap_halves(v)).astype(BF16)
    base += LANES
    for c in range(B_HEADS // 2):
        a = y[:, base + c * LANES: base + (c + 1) * LANES] * qb_scale
        qb_ref[2 * c] = _low_half(a).astype(BF16)
        qb_ref[2 * c + 1] = _low_half(_swap_halves(a)).astype(BF16)
    base += B_HEADS * HEAD_DIM
    kk = y[:, base:base + LANES]
    kb_ref[0] = _low_half(kk).astype(BF16)
    kb_ref[1] = _low_half(_swap_halves(kk)).astype(BF16)
    base += LANES
    v = y[:, base:base + LANES]
    vb_ref[0] = _with_ones_column(v).astype(BF16)
    vb_ref[1] = _with_ones_column(_swap_halves(v)).astype(BF16)


def _rot_half_perm(n_cols, half):
    col = np.arange(n_cols)
    first = (col % (2 * half)) < half
    perm = np.where(first, col + half, col - half)
    sign = np.where(first, -1.0, 1.0).astype(np.float32)
    return perm, sign


def _inproj_even(x, g_mix, w_in, q_gain, k_gain, cos128, sin128, *, tm):
    b, s, d = x.shape
    q_scale = HEAD_DIM ** -0.5 * LOG2E
    w_qk = w_in[:, :QK_A]
    perm, sign = _rot_half_perm(QK_A, HEAD_DIM // 4)
    w_rot = w_qk[:, perm] * sign
    w_all = jnp.concatenate([w_qk, w_rot, w_in[:, QK_A:]], axis=1).astype(BF16)
    gains = jnp.concatenate([jnp.tile(q_gain * q_scale, A_HEADS), jnp.tile(k_gain, A_KV_HEADS)])
    gq = gains.reshape(1, QK_A)
    gr = gains[perm].reshape(1, QK_A)
    blk = np.arange(QK_A) // HEAD_DIM
    avg = jnp.asarray((blk[:, None] == blk[None, :]).astype(np.float32) / HEAD_DIM, BF16)
    n_all = w_all.shape[1]

    def heads(n):
        return pl.BlockSpec((None, n, tm, LANES), lambda bi, i: (bi, 0, i, 0))

    def out(n):
        return jax.ShapeDtypeStruct((b, n, s, LANES), BF16)

    return pl.pallas_call(
        functools.partial(_inproj_even_kernel, qb_scale=q_scale),
        grid=(b, s // tm),
        in_specs=[
            pl.BlockSpec((None, tm, d), lambda bi, i: (bi, i, 0)),
            _const_spec((1, d)),
            _const_spec((d, n_all)),
            _const_spec((QK_A, QK_A)),
            _const_spec((1, QK_A)),
            _const_spec((1, QK_A)),
            pl.BlockSpec((tm, LANES), lambda bi, i: (i, 0)),
            pl.BlockSpec((tm, LANES), lambda bi, i: (i, 0)),
        ],
        out_specs=[heads(A_HEADS), heads(A_KV_HEADS), heads(A_KV_HEADS),
                   heads(B_HEADS), heads(B_KV_HEADS), heads(B_KV_HEADS)],
        out_shape=[out(A_HEADS), out(A_KV_HEADS), out(A_KV_HEADS),
                   out(B_HEADS), out(B_KV_HEADS), out(B_KV_HEADS)],
        compiler_params=_params("parallel", "parallel"),
        name="inproj_even",
    )(x, g_mix.reshape(1, d), w_all, avg, gq, gr, cos128, sin128)


def _inproj_latent_kernel(x_ref, g_ref, wd_ref, gq_ref, gkv_ref, wq_ref, wqr_ref, wk_ref, wv_ref,
                          cos_ref, sin_ref, q_ref, k_ref, v_ref, *, q_scale):
    h = _rms(x_ref[...], g_ref[...]).astype(BF16)
    c = jnp.dot(h, wd_ref[...], preferred_element_type=F32)
    cq = _rms(c[:, :Q_LORA], gq_ref[...]).astype(BF16)
    ckv = _rms(c[:, Q_LORA:Q_LORA + KV_LORA], gkv_ref[...]).astype(BF16)
    base = Q_LORA + KV_LORA
    cos = cos_ref[...]
    sin = sin_ref[...]
    k_rope = c[:, base:base + LANES] * cos + c[:, base + LANES:base + 2 * LANES] * sin
    yq = jnp.dot(cq, wq_ref[...], preferred_element_type=F32)
    yqr = jnp.dot(cq, wqr_ref[...], preferred_element_type=F32)
    yk = jnp.dot(ckv, wk_ref[...], preferred_element_type=F32)
    yv = jnp.dot(ckv, wv_ref[...], preferred_element_type=F32)
    for hd in range(C_HEADS):
        sl = slice(hd * LANES, (hd + 1) * LANES)
        q_ref[hd] = ((yq[:, sl] * cos + yqr[:, sl] * sin) * q_scale).astype(BF16)
        k_ref[hd] = (yk[:, sl] + k_rope).astype(BF16)
        v_ref[hd] = _with_ones_column(yv[:, sl]).astype(BF16)


def _pad_heads(w, n_heads, width, lo, hi, at):
    k = w.shape[0]
    w3 = w.reshape(k, n_heads, width)[:, :, lo:hi]
    w3 = jnp.pad(w3, ((0, 0), (0, 0), (at, LANES - at - (hi - lo))))
    return w3.reshape(k, n_heads * LANES)


def _inproj_latent(x, g_mix, w_down, q_gain, kv_gain, w_uq, w_ukv, cos128, sin128, *, tm):
    b, s, d = x.shape
    qk_dim = NOPE_DIM + ROPE_DIM
    q_scale = qk_dim ** -0.5 * LOG2E
    perm, sign = _rot_half_perm(ROPE_DIM, ROPE_DIM // 2)
    base = Q_LORA + KV_LORA
    w_kr = w_down[:, base:]
    pad = ((0, 0), (NOPE_DIM, LANES - NOPE_DIM - ROPE_DIM))
    wd = jnp.concatenate([w_down[:, :base], jnp.pad(w_kr, pad), jnp.pad(w_kr[:, perm] * sign, pad)],
                         axis=1).astype(BF16)
    wq = _pad_heads(w_uq, C_HEADS, qk_dim, 0, qk_dim, 0).astype(BF16)
    w_qr = w_uq.reshape(Q_LORA, C_HEADS, qk_dim)[:, :, NOPE_DIM:][:, :, perm] * sign
    wqr = _pad_heads(w_qr.reshape(Q_LORA, C_HEADS * ROPE_DIM), C_HEADS, ROPE_DIM, 0, ROPE_DIM,
                     NOPE_DIM).astype(BF16)
    wk = _pad_heads(w_ukv, C_HEADS, NOPE_DIM + V_DIM, 0, NOPE_DIM, 0).astype(BF16)
    wv = _pad_heads(w_ukv, C_HEADS, NOPE_DIM + V_DIM, NOPE_DIM, NOPE_DIM + V_DIM, 0).astype(BF16)
    nq = C_HEADS * LANES
    head_spec = pl.BlockSpec((None, C_HEADS, tm, LANES), lambda bi, i: (bi, 0, i, 0))
    out = jax.ShapeDtypeStruct((b, C_HEADS, s, LANES), BF16)
    return pl.pallas_call(
        functools.partial(_inproj_latent_kernel, q_scale=q_scale),
        grid=(b, s // tm),
        in_specs=[
            pl.BlockSpec((None, tm, d), lambda bi, i: (bi, i, 0)),
            _const_spec((1, d)),
            _const_spec(wd.shape),
            _const_spec((1, Q_LORA)),
            _const_spec((1, KV_LORA)),
            _const_spec((Q_LORA, nq)),
            _const_spec((Q_LORA, nq)),
            _const_spec((KV_LORA, nq)),
            _const_spec((KV_LORA, nq)),
            pl.BlockSpec((tm, LANES), lambda bi, i: (i, 0)),
            pl.BlockSpec((tm, LANES), lambda bi, i: (i, 0)),
        ],
        out_specs=[head_spec, head_spec, head_spec],
        out_shape=[out, out, out],
        compiler_params=_params("parallel", "parallel"),
        name="inproj_latent",
    )(x, g_mix.reshape(1, d), wd, q_gain.reshape(1, Q_LORA), kv_gain.reshape(1, KV_LORA),
      wq, wqr, wk, wv, cos128, sin128)


SCORE_RING = 4
SCORE_AHEAD = 2


def _flash_rows(q, k_ref, v_ref, kv, s_ref, acc_ref, *, tk):
    rows = q.shape[0]
    n_chunks = k_ref.shape[1] // tk
    ring = s_ref.shape[0]
    assert n_chunks % ring == 0 and SCORE_AHEAD < ring
    acc_ref[...] = jnp.zeros_like(acc_ref)

    def scores(j, slot):
        start = pl.multiple_of(j * tk, tk)
        kc = k_ref[kv, pl.ds(start, tk), :]
        s_ref[slot] = lax.dot_general(q, kc, (((1,), (1,)), ((), ())), preferred_element_type=F32)

    def softmax_pv(j, slot, m):
        start = pl.multiple_of(j * tk, tk)
        vc = v_ref[kv, pl.ds(start, tk), :]
        m_new = jnp.maximum(m, jnp.max(s_ref[slot], axis=1, keepdims=True))
        p = jnp.exp2(s_ref[slot] - m_new).astype(BF16)
        alpha = jnp.exp2(m - m_new)
        acc_ref[...] = acc_ref[...] * alpha + jnp.dot(p, vc, preferred_element_type=F32)
        return m_new

    def body(i, m):
        for r in range(ring):
            scores(i * ring + r + SCORE_AHEAD, (r + SCORE_AHEAD) % ring)
            m = softmax_pv(i * ring + r, r, m)
        return m

    for j in range(SCORE_AHEAD):
        scores(j, j)
    m = lax.fori_loop(0, n_chunks // ring - 1, body, jnp.full((rows, 1), -jnp.inf, F32))
    for r in range(ring):
        j = n_chunks - ring + r
        if j + SCORE_AHEAD < n_chunks:
            scores(j + SCORE_AHEAD, (r + SCORE_AHEAD) % ring)
        m = softmax_pv(j, r, m)
    return acc_ref[...]


def _normalise(acc):
    return acc / acc[:, V_DIM:V_DIM + 1]


def _flash_kernel(q_ref, k_ref, v_ref, o_ref, s_ref, acc_ref, *, tq, tk, shared_kv):
    if shared_kv:
        q = q_ref[...].reshape(2 * tq, LANES)
        o = _normalise(_flash_rows(q, k_ref, v_ref, 0, s_ref, acc_ref, tk=tk))
        o0, o1 = o[:tq], o[tq:]
    else:
        o0 = _normalise(_flash_rows(q_ref[0], k_ref, v_ref, 0, s_ref, acc_ref, tk=tk))
        o1 = _normalise(_flash_rows(q_ref[1], k_ref, v_ref, 1, s_ref, acc_ref, tk=tk))
    lane = _lane_ids(o0.shape)
    o_ref[...] = jnp.where(lane < V_DIM, o0, _swap_halves(o1)).astype(o_ref.dtype)


def _flash_attention(q, k, v, *, tq, tk):
    b, h, s, _ = q.shape
    hkv = k.shape[1]
    shared_kv = h // hkv >= 2
    if shared_kv:
        group_pairs = h // hkv // 2
        kv_spec = pl.BlockSpec((None, 1, s, LANES), lambda bi, p, i: (bi, p // group_pairs, 0, 0),
                               pipeline_mode=pl.Buffered(1))
        rows = 2 * tq
    else:
        kv_spec = pl.BlockSpec((None, 2, s, LANES), lambda bi, p, i: (bi, p, 0, 0),
                               pipeline_mode=pl.Buffered(1))
        rows = tq
    return pl.pallas_call(
        functools.partial(_flash_kernel, tq=tq, tk=tk, shared_kv=shared_kv),
        grid=(b, h // 2, s // tq),
        in_specs=[
            pl.BlockSpec((None, 2, tq, LANES), lambda bi, p, i: (bi, p, i, 0)),
            kv_spec,
            kv_spec,
        ],
        out_specs=pl.BlockSpec((None, tq, LANES), lambda bi, p, i: (bi, i, p)),
        out_shape=jax.ShapeDtypeStruct((b, s, h * V_DIM), BF16),
        scratch_shapes=[pltpu.VMEM((SCORE_RING, rows, tk), F32), pltpu.VMEM((rows, LANES), F32)],
        compiler_params=_params("parallel", "parallel", "arbitrary"),
        name="flash_shared" if shared_kv else "flash_mha",
    )(q, k, v)


def _window_kernel(sink_ref, q_ref, kp_ref, km_ref, kn_ref, vp_ref, vm_ref, vn_ref, o_ref,
                   *, tq, seq, group):
    g = pl.program_id(1)
    i = pl.program_id(2)
    kcat = jnp.concatenate([kp_ref[...], km_ref[...], kn_ref[...]], axis=0)
    vcat = jnp.concatenate([vp_ref[...], vm_ref[...], vn_ref[...]], axis=0)
    nk = tq + 2 * WINDOW
    q = q_ref[...].reshape(group * tq, LANES)
    s_all = lax.dot_general(q, kcat, (((1,), (1,)), ((), ())), preferred_element_type=F32)
    row = lax.broadcasted_iota(jnp.int32, (tq, nk), 0)
    col = lax.broadcasted_iota(jnp.int32, (tq, nk), 1)
    dist = jnp.abs(col - WINDOW - row)
    kpos = i * tq - WINDOW + col
    valid = (dist <= WINDOW) & (kpos >= 0) & (kpos < seq)
    dist_f = dist.astype(F32)
    outs = []
    for hh in range(group):
        head = g * group + hh
        slope = jnp.exp2(-8.0 * (head + 1).astype(F32) / B_HEADS) * LOG2E
        sink = sink_ref[head] * LOG2E
        sc = jnp.where(valid, s_all[hh * tq:(hh + 1) * tq] - slope * dist_f, -jnp.inf)
        m = jnp.maximum(jnp.max(sc, axis=1, keepdims=True), sink)
        e = jnp.exp2(sc - m).astype(BF16)
        acc = jnp.dot(e, vcat, preferred_element_type=F32)
        denom = acc[:, V_DIM:V_DIM + 1] + jnp.exp2(sink - m)
        outs.append(acc / denom)
    lane = _lane_ids(outs[0].shape)
    pairs = [jnp.where(lane < V_DIM, outs[2 * c], _swap_halves(outs[2 * c + 1]))
             for c in range(group // 2)]
    o_ref[...] = jnp.concatenate(pairs, axis=1).astype(o_ref.dtype)


def _window_attention(q, k, v, sink, *, tq):
    b, h, s, _ = q.shape
    hkv = k.shape[1]
    group = h // hkv
    per = tq // WINDOW
    last = s // WINDOW - 1

    def main(bi, g, i):
        return (bi, g, i, 0)

    def prev(bi, g, i):
        return (bi, g, jnp.maximum(i * per - 1, 0), 0)

    def nxt(bi, g, i):
        return (bi, g, jnp.minimum((i + 1) * per, last), 0)

    halo = lambda f: pl.BlockSpec((None, None, WINDOW, LANES), f)
    mid = pl.BlockSpec((None, None, tq, LANES), main)
    return pl.pallas_call(
        functools.partial(_window_kernel, tq=tq, seq=s, group=group),
        grid=(b, hkv, s // tq),
        in_specs=[
            pl.BlockSpec(memory_space=pltpu.SMEM),
            pl.BlockSpec((None, group, tq, LANES), main),
            halo(prev), mid, halo(nxt),
            halo(prev), mid, halo(nxt),
        ],
        out_specs=pl.BlockSpec((None, tq, group * V_DIM), lambda bi, g, i: (bi, i, g)),
        out_shape=jax.ShapeDtypeStruct((b, s, h * V_DIM), BF16),
        compiler_params=_params("parallel", "parallel", "parallel"),
        name="window_attention",
    )(sink, q, k, k, k, v, v, v)


def _outproj_kernel(x_ref, *refs):
    o_ref = refs[-1]
    acc = x_ref[...]
    n = (len(refs) - 1) // 2
    for a_ref, w_ref in zip(refs[:n], refs[n:2 * n]):
        acc = acc + jnp.dot(a_ref[...], w_ref[...], preferred_element_type=F32)
    o_ref[...] = acc


def _outproj(x, acts, weights, *, tm):
    b, s, d = x.shape
    row = lambda w: pl.BlockSpec((None, tm, w), lambda bi, i: (bi, i, 0))
    return pl.pallas_call(
        _outproj_kernel,
        grid=(b, s // tm),
        in_specs=[row(d)] + [row(a.shape[-1]) for a in acts] + [_const_spec(w.shape) for w in weights],
        out_specs=row(d),
        out_shape=jax.ShapeDtypeStruct((b, s, d), F32),
        compiler_params=_params("parallel", "parallel"),
        name="outproj",
    )(x, *acts, *weights)


FF_CHUNK = D_FF // 2
HALO = 8


def _ffn_kernel(xp_ref, x_ref, xn_ref, g_ref, wg_ref, wv_ref, cw_ref, cb_ref, wd_ref, gf_ref, o_ref,
                *, tm, final_norm):
    i = pl.program_id(1)
    n_blocks = pl.num_programs(1)
    x = x_ref[...]
    xp = jnp.where(i > 0, xp_ref[...], 0.0)
    xn = jnp.where(i < n_blocks - 1, xn_ref[...], 0.0)
    x_ext = jnp.concatenate([xp, x, xn], axis=0)
    h = _rms(x_ext, g_ref[...]).astype(BF16)
    ext = tm + 2 * HALO
    acc = x
    for c in range(D_FF // FF_CHUNK):
        sl = slice(c * FF_CHUNK, (c + 1) * FF_CHUNK)
        gate = jnp.dot(h, wg_ref[:, sl], preferred_element_type=F32)
        val = jnp.dot(h, wv_ref[:, sl], preferred_element_type=F32)[HALO:HALO + tm]
        g_prev = pltpu.roll(gate, 1, axis=0)[HALO:HALO + tm]
        g_next = pltpu.roll(gate, ext - 1, axis=0)[HALO:HALO + tm]
        gc = (cb_ref[:, sl] + g_prev * cw_ref[0:1, sl] + gate[HALO:HALO + tm] * cw_ref[1:2, sl]
              + g_next * cw_ref[2:3, sl])
        act = (gc * jax.nn.sigmoid(gc) * val).astype(BF16)
        acc = acc + jnp.dot(act, wd_ref[sl, :], preferred_element_type=F32)
    if final_norm:
        acc = _rms(acc, gf_ref[...])
    o_ref[...] = acc


def _ffn(x, g_ffn, w_up, conv_w, conv_b, w_down, g_final, *, tm, final_norm):
    b, s, d = x.shape
    per = tm // HALO
    last = s // HALO - 1
    wg = w_up[:, :D_FF].astype(BF16)
    wv = w_up[:, D_FF:].astype(BF16)
    wd = w_down.astype(BF16)
    halo = lambda f: pl.BlockSpec((None, HALO, d), f)
    return pl.pallas_call(
        functools.partial(_ffn_kernel, tm=tm, final_norm=final_norm),
        grid=(b, s // tm),
        in_specs=[
            halo(lambda bi, i: (bi, jnp.maximum(i * per - 1, 0), 0)),
            pl.BlockSpec((None, tm, d), lambda bi, i: (bi, i, 0)),
            halo(lambda bi, i: (bi, jnp.minimum((i + 1) * per, last), 0)),
            _const_spec((1, d)),
            _const_spec(wg.shape),
            _const_spec(wv.shape),
            _const_spec(conv_w.shape),
            _const_spec((1, D_FF)),
            _const_spec(wd.shape),
            _const_spec((1, d)),
        ],
        out_specs=pl.BlockSpec((None, tm, d), lambda bi, i: (bi, i, 0)),
        out_shape=jax.ShapeDtypeStruct((b, s, d), F32),
        compiler_params=_params("parallel", "parallel"),
        name="ffn_final" if final_norm else "ffn",
    )(x, x, x, g_ffn.reshape(1, d), wg, wv, conv_w, conv_b.reshape(1, D_FF), wd, g_final.reshape(1, d))


def _rope_angles(pos, dim):
    inv_freq = ROPE_THETA ** (-jnp.arange(0, dim, 2, dtype=F32) / dim)
    ang = pos.astype(F32)[:, None] * inv_freq[None, :]
    return jnp.cos(ang), jnp.sin(ang)


def _rope_tables(s):
    t = jnp.arange(s)
    cos_r, sin_r = _rope_angles(t // GRID_W, HEAD_DIM // 2)
    cos_c, sin_c = _rope_angles(t % GRID_W, HEAD_DIM // 2)
    cos_ax = jnp.tile(jnp.concatenate([cos_r, cos_r, cos_c, cos_c], axis=1), (1, 2))
    sin_ax = jnp.tile(jnp.concatenate([sin_r, sin_r, sin_c, sin_c], axis=1), (1, 2))
    cos_t, sin_t = _rope_angles(t, ROPE_DIM)
    tail = jnp.zeros((s, LANES - NOPE_DIM - ROPE_DIM), F32)
    cos_l = jnp.concatenate([jnp.ones((s, NOPE_DIM), F32), cos_t, cos_t, tail], axis=1)
    sin_l = jnp.concatenate([jnp.zeros((s, NOPE_DIM), F32), sin_t, sin_t, tail], axis=1)
    return cos_ax, sin_ax, cos_l, sin_l


TM = 512
TQ_SHARED = 512
TQ_MHA = 1024
TK = 1024
TQ_WINDOW = 256


def _trunk(x, p, tables):
    cos_ax, sin_ax, cos_l, sin_l = tables
    s = x.shape[1]
    tm = min(TM, s)
    qa, ka, va, qb, kb, vb = _inproj_even(x, p["norm_mix"][0], p["e_w_in"][0], p["e_q_gain"][0],
                                          p["e_k_gain"][0], cos_ax, sin_ax, tm=tm)
    oa = _flash_attention(qa, ka, va, tq=min(TQ_SHARED, s), tk=min(TK, s // (2 * SCORE_RING)))
    ob = _window_attention(qb, kb, vb, p["e_sink"][0], tq=min(TQ_WINDOW, s))
    w_out = p["e_w_out"][0].astype(BF16)
    n_a = A_HEADS * HEAD_DIM
    x = _outproj(x, [oa, ob], [w_out[:n_a], w_out[n_a:]], tm=tm)
    x = _ffn(x, p["norm_ffn"][0], p["f_w_up"][0], p["f_conv_w"][0], p["f_conv_b"][0], p["f_w_down"][0],
             p["norm_final"], tm=tm, final_norm=False)
    q, k, v = _inproj_latent(x, p["norm_mix"][1], p["o_w_down"][0], p["o_q_gain"][0], p["o_kv_gain"][0],
                             p["o_w_uq"][0], p["o_w_ukv"][0], cos_l, sin_l, tm=tm)
    o = _flash_attention(q, k, v, tq=min(TQ_MHA, s), tk=min(TK, s // (2 * SCORE_RING)))
    x = _outproj(x, [o], [p["o_w_out"][0].astype(BF16)], tm=tm)
    x = _ffn(x, p["norm_ffn"][1], p["f_w_up"][1], p["f_conv_w"][1], p["f_conv_b"][1], p["f_w_down"][1],
             p["norm_final"], tm=tm, final_norm=True)
    return x


def kernel(x_prompt, x_sample, norm_mix, norm_ffn, norm_final, e_w_in, e_q_gain, e_k_gain, e_sink, e_w_out, o_w_down, o_q_gain, o_kv_gain, o_w_uq, o_w_ukv, o_w_out, f_w_up, f_conv_w, f_conv_b, f_w_down):
    p = dict(norm_mix=norm_mix, norm_ffn=norm_ffn, norm_final=norm_final, e_w_in=e_w_in, e_q_gain=e_q_gain,
             e_k_gain=e_k_gain, e_sink=e_sink, e_w_out=e_w_out, o_w_down=o_w_down, o_q_gain=o_q_gain,
             o_kv_gain=o_kv_gain, o_w_uq=o_w_uq, o_w_ukv=o_w_ukv, o_w_out=o_w_out, f_w_up=f_w_up,
             f_conv_w=f_conv_w, f_conv_b=f_conv_b, f_w_down=f_w_down)
    tables = _rope_tables(x_prompt.shape[1])
    return _trunk(x_prompt, p, tables), _trunk(x_sample, p, tables)
```

```python
import functools
import math

import numpy as np
import jax
import jax.numpy as jnp
from jax import lax
from jax.experimental import pallas as pl
from jax.experimental.pallas import tpu as pltpu

D_MODEL = 1024
GRID_W = 64
HEAD_DIM = 64
A_HEADS = 8
A_KV_HEADS = 2
B_HEADS = 8
B_KV_HEADS = 2
WINDOW = 128
C_HEADS = 16
Q_LORA = 384
KV_LORA = 256
NOPE_DIM = 64
ROPE_DIM = 32
V_DIM = 64
D_FF = 2816
CONV_W = 3
ROPE_THETA = 10000.0
EPS = 1e-6

LANES = 128
LOG2E = math.log2(math.e)
VMEM_LIMIT = 56 * 1024 * 1024

F32 = jnp.float32
BF16 = jnp.bfloat16

QK_A = (A_HEADS + A_KV_HEADS) * HEAD_DIM


def _params(*sem):
    return pltpu.CompilerParams(dimension_semantics=sem, vmem_limit_bytes=VMEM_LIMIT)


def _const_spec(shape):
    nd = len(shape)
    return pl.BlockSpec(shape, lambda *_: (0,) * nd, pipeline_mode=pl.Buffered(1))


def _rms(x, gain):
    ms = jnp.mean(x * x, axis=-1, keepdims=True)
    return x * lax.rsqrt(ms + EPS) * gain


def _lane_ids(shape):
    return lax.broadcasted_iota(jnp.int32, shape, len(shape) - 1)


def _low_half(x):
    return jnp.where(_lane_ids(x.shape) < HEAD_DIM, x, 0.0)


def _swap_halves(x):
    return pltpu.roll(x, HEAD_DIM, axis=1)


def _with_ones_column(v):
    lane = _lane_ids(v.shape)
    return jnp.where(lane < V_DIM, v, jnp.where(lane == V_DIM, 1.0, 0.0))


def _inproj_even_kernel(x_ref, g_ref, w_ref, avg_ref, gq_ref, gr_ref, cos_ref, sin_ref,
                        qa_ref, ka_ref, va_ref, qb_ref, kb_ref, vb_ref, *, qb_scale):
    h = _rms(x_ref[...], g_ref[...]).astype(BF16)
    y = jnp.dot(h, w_ref[...], preferred_element_type=F32)
    yqk = y[:, :QK_A]
    yrot = y[:, QK_A:2 * QK_A]
    sq = yqk * yqk
    hi = sq.astype(BF16)
    lo = (sq - hi.astype(F32)).astype(BF16)
    msq = (jnp.dot(hi, avg_ref[...], preferred_element_type=F32)
           + jnp.dot(lo, avg_ref[...], preferred_element_type=F32))
    r = lax.rsqrt(msq + EPS)
    cos = cos_ref[...]
    sin = sin_ref[...]
    for c in range(QK_A // LANES):
        sl = slice(c * LANES, (c + 1) * LANES)
        a = (yqk[:, sl] * (gq_ref[:, sl] * cos) + yrot[:, sl] * (gr_ref[:, sl] * sin)) * r[:, sl]
        even = _low_half(a).astype(BF16)
        odd = _low_half(_swap_halves(a)).astype(BF16)
        if c < A_HEADS // 2:
            qa_ref[2 * c] = even
            qa_ref[2 * c + 1] = odd
        else:
            ka_ref[0] = even
            ka_ref[1] = odd
    base = 2 * QK_A
    v = y[:, base:base + LANES]
    va_ref[0] = _with_ones_column(v).astype(BF16)
    va_ref[1] = _with_ones_column(_swap_halves(v)).astype(BF16)
    base += LANES
    for c in range(B_HEADS // 2):
        a = y[:, base + c * LANES: base + (c + 1) * LANES] * qb_scale
        qb_ref[2 * c] = _low_half(a).astype(BF16)
        qb_ref[2 * c + 1] = _low_half(_swap_halves(a)).astype(BF16)
    base += B_HEADS * HEAD_DIM
    kk = y[:, base:base + LANES]
    kb_ref[0] = _low_half(kk).astype(BF16)
    kb_ref[1] = _low_half(_swap_halves(kk)).astype(BF16)
    base += LANES
    v = y[:, base:base + LANES]
    vb_ref[0] = _with_ones_column(v).astype(BF16)
    vb_ref[1] = _with_ones_column(_swap_halves(v)).astype(BF16)


def _rot_half_perm(n_cols, half):
    col = np.arange(n_cols)
    first = (col % (2 * half)) < half
    perm = np.where(first, col + half, col - half)
    sign = np.where(first, -1.0, 1.0).astype(np.float32)
    return perm, sign


def _inproj_even(x, g_mix, w_in, q_gain, k_gain, cos128, sin128, *, tm):
    b, s, d = x.shape
    q_scale = HEAD_DIM ** -0.5 * LOG2E
    w_qk = w_in[:, :QK_A]
    perm, sign = _rot_half_perm(QK_A, HEAD_DIM // 4)
    w_rot = w_qk[:, perm] * sign
    w_all = jnp.concatenate([w_qk, w_rot, w_in[:, QK_A:]], axis=1).astype(BF16)
    gains = jnp.concatenate([jnp.tile(q_gain * q_scale, A_HEADS), jnp.tile(k_gain, A_KV_HEADS)])
    gq = gains.reshape(1, QK_A)
    gr = gains[perm].reshape(1, QK_A)
    blk = np.arange(QK_A) // HEAD_DIM
    avg = jnp.asarray((blk[:, None] == blk[None, :]).astype(np.float32) / HEAD_DIM, BF16)
    n_all = w_all.shape[1]

    def heads(n):
        return pl.BlockSpec((None, n, tm, LANES), lambda bi, i: (bi, 0, i, 0))

    def out(n):
        return jax.ShapeDtypeStruct((b, n, s, LANES), BF16)

    return pl.pallas_call(
        functools.partial(_inproj_even_kernel, qb_scale=q_scale),
        grid=(b, s // tm),
        in_specs=[
            pl.BlockSpec((None, tm, d), lambda bi, i: (bi, i, 0)),
            _const_spec((1, d)),
            _const_spec((d, n_all)),
            _const_spec((QK_A, QK_A)),
            _const_spec((1, QK_A)),
            _const_spec((1, QK_A)),
            pl.BlockSpec((tm, LANES), lambda bi, i: (i, 0)),
            pl.BlockSpec((tm, LANES), lambda bi, i: (i, 0)),
        ],
        out_specs=[heads(A_HEADS), heads(A_KV_HEADS), heads(A_KV_HEADS),
                   heads(B_HEADS), heads(B_KV_HEADS), heads(B_KV_HEADS)],
        out_shape=[out(A_HEADS), out(A_KV_HEADS), out(A_KV_HEADS),
                   out(B_HEADS), out(B_KV_HEADS), out(B_KV_HEADS)],
        compiler_params=_params("parallel", "parallel"),
        name="inproj_even",
    )(x, g_mix.reshape(1, d), w_all, avg, gq, gr, cos128, sin128)


def _inproj_latent_kernel(x_ref, g_ref, wd_ref, gq_ref, gkv_ref, wq_ref, wqr_ref, wk_ref, wv_ref,
                          cos_ref, sin_ref, q_ref, k_ref, v_ref, *, q_scale):
    h = _rms(x_ref[...], g_ref[...]).astype(BF16)
    c = jnp.dot(h, wd_ref[...], preferred_element_type=F32)
    cq = _rms(c[:, :Q_LORA], gq_ref[...]).astype(BF16)
    ckv = _rms(c[:, Q_LORA:Q_LORA + KV_LORA], gkv_ref[...]).astype(BF16)
    base = Q_LORA + KV_LORA
    cos = cos_ref[...]
    sin = sin_ref[...]
    k_rope = c[:, base:base + LANES] * cos + c[:, base + LANES:base + 2 * LANES] * sin
    yq = jnp.dot(cq, wq_ref[...], preferred_element_type=F32)
    yqr = jnp.dot(cq, wqr_ref[...], preferred_element_type=F32)
    yk = jnp.dot(ckv, wk_ref[...], preferred_element_type=F32)
    yv = jnp.dot(ckv, wv_ref[...], preferred_element_type=F32)
    for hd in range(C_HEADS):
        sl = slice(hd * LANES, (hd + 1) * LANES)
        q_ref[hd] = ((yq[:, sl] * cos + yqr[:, sl] * sin) * q_scale).astype(BF16)
        k_ref[hd] = (yk[:, sl] + k_rope).astype(BF16)
        v_ref[hd] = _with_ones_column(yv[:, sl]).astype(BF16)


def _pad_heads(w, n_heads, width, lo, hi, at):
    k = w.shape[0]
    w3 = w.reshape(k, n_heads, width)[:, :, lo:hi]
    w3 = jnp.pad(w3, ((0, 0), (0, 0), (at, LANES - at - (hi - lo))))
    return w3.reshape(k, n_heads * LANES)


def _inproj_latent(x, g_mix, w_down, q_gain, kv_gain, w_uq, w_ukv, cos128, sin128, *, tm):
    b, s, d = x.shape
    qk_dim = NOPE_DIM + ROPE_DIM
    q_scale = qk_dim ** -0.5 * LOG2E
    perm, sign = _rot_half_perm(ROPE_DIM, ROPE_DIM // 2)
    base = Q_LORA + KV_LORA
    w_kr = w_down[:, base:]
    pad = ((0, 0), (NOPE_DIM, LANES - NOPE_DIM - ROPE_DIM))
    wd = jnp.concatenate([w_down[:, :base], jnp.pad(w_kr, pad), jnp.pad(w_kr[:, perm] * sign, pad)],
                         axis=1).astype(BF16)
    wq = _pad_heads(w_uq, C_HEADS, qk_dim, 0, qk_dim, 0).astype(BF16)
    w_qr = w_uq.reshape(Q_LORA, C_HEADS, qk_dim)[:, :, NOPE_DIM:][:, :, perm] * sign
    wqr = _pad_heads(w_qr.reshape(Q_LORA, C_HEADS * ROPE_DIM), C_HEADS, ROPE_DIM, 0, ROPE_DIM,
                     NOPE_DIM).astype(BF16)
    wk = _pad_heads(w_ukv, C_HEADS, NOPE_DIM + V_DIM, 0, NOPE_DIM, 0).astype(BF16)
    wv = _pad_heads(w_ukv, C_HEADS, NOPE_DIM + V_DIM, NOPE_DIM, NOPE_DIM + V_DIM, 0).astype(BF16)
    nq = C_HEADS * LANES
    head_spec = pl.BlockSpec((None, C_HEADS, tm, LANES), lambda bi, i: (bi, 0, i, 0))
    out = jax.ShapeDtypeStruct((b, C_HEADS, s, LANES), BF16)
    return pl.pallas_call(
        functools.partial(_inproj_latent_kernel, q_scale=q_scale),
        grid=(b, s // tm),
        in_specs=[
            pl.BlockSpec((None, tm, d), lambda bi, i: (bi, i, 0)),
            _const_spec((1, d)),
            _const_spec(wd.shape),
            _const_spec((1, Q_LORA)),
            _const_spec((1, KV_LORA)),
            _const_spec((Q_LORA, nq)),
            _const_spec((Q_LORA, nq)),
            _const_spec((KV_LORA, nq)),
            _const_spec((KV_LORA, nq)),
            pl.BlockSpec((tm, LANES), lambda bi, i: (i, 0)),
            pl.BlockSpec((tm, LANES), lambda bi, i: (i, 0)),
        ],
        out_specs=[head_spec, head_spec, head_spec],
        out_shape=[out, out, out],
        compiler_params=_params("parallel", "parallel"),
        name="inproj_latent",
    )(x, g_mix.reshape(1, d), wd, q_gain.reshape(1, Q_LORA), kv_gain.reshape(1, KV_LORA),
      wq, wqr, wk, wv, cos128, sin128)


SCORE_RING = 4
SCORE_AHEAD = 2


def _flash_rows(q, k_ref, v_ref, kv, s_ref, acc_ref, *, tk):
    rows = q.shape[0]
    n_chunks = k_ref.shape[1] // tk
    ring = s_ref.shape[0]
    assert SCORE_AHEAD < ring <= n_chunks
    n_trips = (n_chunks - SCORE_AHEAD) // ring
    acc_ref[...] = jnp.zeros_like(acc_ref)

    def scores(j, slot):
        start = pl.multiple_of(j * tk, tk)
        kc = k_ref[kv, pl.ds(start, tk), :]
        s_ref[slot] = lax.dot_general(q, kc, (((1,), (1,)), ((), ())), preferred_element_type=F32)

    def softmax_pv(j, slot, m):
        start = pl.multiple_of(j * tk, tk)
        vc = v_ref[kv, pl.ds(start, tk), :]
        m_new = jnp.maximum(m, jnp.max(s_ref[slot], axis=1, keepdims=True))
        p = jnp.exp2(s_ref[slot] - m_new).astype(BF16)
        alpha = jnp.exp2(m - m_new)
        acc_ref[...] = acc_ref[...] * alpha + jnp.dot(p, vc, preferred_element_type=F32)
        return m_new

    def body(i, m):
        for r in range(ring):
            scores(i * ring + r + SCORE_AHEAD, (r + SCORE_AHEAD) % ring)
            m = softmax_pv(i * ring + r, r, m)
        return m

    for j in range(SCORE_AHEAD):
        scores(j, j)
    m = lax.fori_loop(0, n_trips, body, jnp.full((rows, 1), -jnp.inf, F32))
    for j in range(n_trips * ring, n_chunks):
        if j + SCORE_AHEAD < n_chunks:
            scores(j + SCORE_AHEAD, (j + SCORE_AHEAD) % ring)
        m = softmax_pv(j, j % ring, m)
    return acc_ref[...]


def _normalise(acc):
    return acc / acc[:, V_DIM:V_DIM + 1]


def _flash_kernel(q_ref, k_ref, v_ref, o_ref, s_ref, acc_ref, *, tq, tk, shared_kv):
    if shared_kv:
        q = q_ref[...].reshape(2 * tq, LANES)
        o = _normalise(_flash_rows(q, k_ref, v_ref, 0, s_ref, acc_ref, tk=tk))
        o0, o1 = o[:tq], o[tq:]
    else:
        o0 = _normalise(_flash_rows(q_ref[0], k_ref, v_ref, 0, s_ref, acc_ref, tk=tk))
        o1 = _normalise(_flash_rows(q_ref[1], k_ref, v_ref, 1, s_ref, acc_ref, tk=tk))
    lane = _lane_ids(o0.shape)
    o_ref[...] = jnp.where(lane < V_DIM, o0, _swap_halves(o1)).astype(o_ref.dtype)


def _flash_attention(q, k, v, *, tq, tk):
    b, h, s, _ = q.shape
    hkv = k.shape[1]
    shared_kv = h // hkv >= 2
    if shared_kv:
        group_pairs = h // hkv // 2
        kv_spec = pl.BlockSpec((None, 1, s, LANES), lambda bi, p, i: (bi, p // group_pairs, 0, 0),
                               pipeline_mode=pl.Buffered(1))
        rows = 2 * tq
    else:
        kv_spec = pl.BlockSpec((None, 2, s, LANES), lambda bi, p, i: (bi, p, 0, 0),
                               pipeline_mode=pl.Buffered(1))
        rows = tq
    return pl.pallas_call(
        functools.partial(_flash_kernel, tq=tq, tk=tk, shared_kv=shared_kv),
        grid=(b, h // 2, s // tq),
        in_specs=[
            pl.BlockSpec((None, 2, tq, LANES), lambda bi, p, i: (bi, p, i, 0)),
            kv_spec,
            kv_spec,
        ],
        out_specs=pl.BlockSpec((None, tq, LANES), lambda bi, p, i: (bi, i, p)),
        out_shape=jax.ShapeDtypeStruct((b, s, h * V_DIM), BF16),
        scratch_shapes=[pltpu.VMEM((SCORE_RING, rows, tk), F32), pltpu.VMEM((rows, LANES), F32)],
        compiler_params=_params("parallel", "parallel", "arbitrary"),
        name="flash_shared" if shared_kv else "flash_mha",
    )(q, k, v)


def _window_kernel(sink_ref, q_ref, kp_ref, km_ref, kn_ref, vp_ref, vm_ref, vn_ref, o_ref,
                   *, tq, seq, group):
    g = pl.program_id(1)
    i = pl.program_id(2)
    kcat = jnp.concatenate([kp_ref[...], km_ref[...], kn_ref[...]], axis=0)
    vcat = jnp.concatenate([vp_ref[...], vm_ref[...], vn_ref[...]], axis=0)
    nk = tq + 2 * WINDOW
    q = q_ref[...].reshape(group * tq, LANES)
    s_all = lax.dot_general(q, kcat, (((1,), (1,)), ((), ())), preferred_element_type=F32)
    row = lax.broadcasted_iota(jnp.int32, (tq, nk), 0)
    col = lax.broadcasted_iota(jnp.int32, (tq, nk), 1)
    dist = jnp.abs(col - WINDOW - row)
    kpos = i * tq - WINDOW + col
    valid = (dist <= WINDOW) & (kpos >= 0) & (kpos < seq)
    dist_f = dist.astype(F32)
    outs = []
    for hh in range(group):
        head = g * group + hh
        slope = jnp.exp2(-8.0 * (head + 1).astype(F32) / B_HEADS) * LOG2E
        sink = sink_ref[head] * LOG2E
        sc = jnp.where(valid, s_all[hh * tq:(hh + 1) * tq] - slope * dist_f, -jnp.inf)
        m = jnp.maximum(jnp.max(sc, axis=1, keepdims=True), sink)
        e = jnp.exp2(sc - m).astype(BF16)
        acc = jnp.dot(e, vcat, preferred_element_type=F32)
        denom = acc[:, V_DIM:V_DIM + 1] + jnp.exp2(sink - m)
        outs.append(acc / denom)
    lane = _lane_ids(outs[0].shape)
    pairs = [jnp.where(lane < V_DIM, outs[2 * c], _swap_halves(outs[2 * c + 1]))
             for c in range(group // 2)]
    o_ref[...] = jnp.concatenate(pairs, axis=1).astype(o_ref.dtype)


def _window_attention(q, k, v, sink, *, tq):
    b, h, s, _ = q.shape
    hkv = k.shape[1]
    group = h // hkv
    per = tq // WINDOW
    last = s // WINDOW - 1

    def main(bi, g, i):
        return (bi, g, i, 0)

    def prev(bi, g, i):
        return (bi, g, jnp.maximum(i * per - 1, 0), 0)

    def nxt(bi, g, i):
        return (bi, g, jnp.minimum((i + 1) * per, last), 0)

    halo = lambda f: pl.BlockSpec((None, None, WINDOW, LANES), f)
    mid = pl.BlockSpec((None, None, tq, LANES), main)
    return pl.pallas_call(
        functools.partial(_window_kernel, tq=tq, seq=s, group=group),
        grid=(b, hkv, s // tq),
        in_specs=[
            pl.BlockSpec(memory_space=pltpu.SMEM),
            pl.BlockSpec((None, group, tq, LANES), main),
            halo(prev), mid, halo(nxt),
            halo(prev), mid, halo(nxt),
        ],
        out_specs=pl.BlockSpec((None, tq, group * V_DIM), lambda bi, g, i: (bi, i, g)),
        out_shape=jax.ShapeDtypeStruct((b, s, h * V_DIM), BF16),
        compiler_params=_params("parallel", "parallel", "parallel"),
        name="window_attention",
    )(sink, q, k, k, k, v, v, v)


HALO = 16


def _mix_ffn_kernel(*refs, tm, n_acts, final_norm):
    x_refs = refs[0:3]
    act_refs = [refs[3 + 3 * a: 6 + 3 * a] for a in range(n_acts)]
    base = 3 + 3 * n_acts
    w_refs = refs[base:base + n_acts]
    g_ref, wg_ref, wv_ref, cw_ref, cb_ref, wd_ref, gf_ref, o_ref = refs[base + n_acts:]
    i = pl.program_id(1)
    ext = tm + 2 * HALO
    x1 = jnp.concatenate([r[...] for r in x_refs], axis=0)
    for triple, w_ref in zip(act_refs, w_refs):
        a = jnp.concatenate([r[...] for r in triple], axis=0)
        x1 = x1 + jnp.dot(a, w_ref[...], preferred_element_type=F32)
    row = lax.broadcasted_iota(jnp.int32, (ext, 1), 0)
    inside = ((row >= HALO) | (i > 0)) & ((row < HALO + tm) | (i < pl.num_programs(1) - 1))
    x1 = jnp.where(inside, x1, 0.0)
    h = _rms(x1, g_ref[...]).astype(BF16)
    mid = slice(HALO, HALO + tm)
    gate = jnp.dot(h, wg_ref[...], preferred_element_type=F32)
    val = jnp.dot(h, wv_ref[...], preferred_element_type=F32)[mid]
    g_prev = pltpu.roll(gate, 1, axis=0)[mid]
    g_next = pltpu.roll(gate, ext - 1, axis=0)[mid]
    gc = cb_ref[...] + g_prev * cw_ref[0:1, :] + gate[mid] * cw_ref[1:2, :] + g_next * cw_ref[2:3, :]
    act = (gc * jax.nn.sigmoid(gc) * val).astype(BF16)
    out = x1[mid] + jnp.dot(act, wd_ref[...], preferred_element_type=F32)
    if final_norm:
        out = _rms(out, gf_ref[...])
    o_ref[...] = out


def _mix_ffn(x, acts, w_outs, g_ffn, w_up, conv_w, conv_b, w_down, g_final, *, tm, final_norm):
    b, s, d = x.shape
    per = tm // HALO
    last = s // HALO - 1
    wg = w_up[:, :D_FF].astype(BF16)
    wv = w_up[:, D_FF:].astype(BF16)
    wd = w_down.astype(BF16)

    def triple(width):
        return [
            pl.BlockSpec((None, HALO, width), lambda bi, i: (bi, jnp.maximum(i * per - 1, 0), 0)),
            pl.BlockSpec((None, tm, width), lambda bi, i: (bi, i, 0)),
            pl.BlockSpec((None, HALO, width), lambda bi, i: (bi, jnp.minimum((i + 1) * per, last), 0)),
        ]

    in_specs = triple(d)
    operands = [x, x, x]
    for a in acts:
        in_specs += triple(a.shape[-1])
        operands += [a, a, a]
    consts = list(w_outs) + [g_ffn.reshape(1, d), wg, wv, conv_w, conv_b.reshape(1, D_FF), wd,
                             g_final.reshape(1, d)]
    in_specs += [_const_spec(c.shape) for c in consts]
    return pl.pallas_call(
        functools.partial(_mix_ffn_kernel, tm=tm, n_acts=len(acts), final_norm=final_norm),
        grid=(b, s // tm),
        in_specs=in_specs,
        out_specs=pl.BlockSpec((None, tm, d), lambda bi, i: (bi, i, 0)),
        out_shape=jax.ShapeDtypeStruct((b, s, d), F32),
        compiler_params=_params("parallel", "parallel"),
        name="mix_ffn_final" if final_norm else "mix_ffn",
    )(*operands, *consts)


def _rope_angles(pos, dim):
    inv_freq = ROPE_THETA ** (-jnp.arange(0, dim, 2, dtype=F32) / dim)
    ang = pos.astype(F32)[:, None] * inv_freq[None, :]
    return jnp.cos(ang), jnp.sin(ang)


def _rope_tables(s):
    t = jnp.arange(s)
    n_rows = s // GRID_W
    cos_r, sin_r = (jnp.repeat(a, GRID_W, axis=0) for a in _rope_angles(jnp.arange(n_rows), HEAD_DIM // 2))
    cos_c, sin_c = (jnp.tile(a, (n_rows, 1)) for a in _rope_angles(jnp.arange(GRID_W), HEAD_DIM // 2))
    cos_ax = jnp.tile(jnp.concatenate([cos_r, cos_r, cos_c, cos_c], axis=1), (1, 2))
    sin_ax = jnp.tile(jnp.concatenate([sin_r, sin_r, sin_c, sin_c], axis=1), (1, 2))
    cos_t, sin_t = _rope_angles(t, ROPE_DIM)
    tail = jnp.zeros((s, LANES - NOPE_DIM - ROPE_DIM), F32)
    cos_l = jnp.concatenate([jnp.ones((s, NOPE_DIM), F32), cos_t, cos_t, tail], axis=1)
    sin_l = jnp.concatenate([jnp.zeros((s, NOPE_DIM), F32), sin_t, sin_t, tail], axis=1)
    return cos_ax, sin_ax, cos_l, sin_l


TM = 512
TQ_SHARED = 512
TQ_MHA = 1024
TK = 1024
TQ_WINDOW = 256


def _trunk(x, p, tables):
    cos_ax, sin_ax, cos_l, sin_l = tables
    s = x.shape[1]
    tm = min(TM, s)
    qa, ka, va, qb, kb, vb = _inproj_even(x, p["norm_mix"][0], p["e_w_in"][0], p["e_q_gain"][0],
                                          p["e_k_gain"][0], cos_ax, sin_ax, tm=tm)
    oa = _flash_attention(qa, ka, va, tq=min(TQ_SHARED, s), tk=min(TK, s // 8))
    ob = _window_attention(qb, kb, vb, p["e_sink"][0], tq=min(TQ_WINDOW, s))
    w_out = p["e_w_out"][0].astype(BF16)
    n_a = A_HEADS * HEAD_DIM
    x = _mix_ffn(x, [oa, ob], [w_out[:n_a], w_out[n_a:]], p["norm_ffn"][0], p["f_w_up"][0], p["f_conv_w"][0],
                 p["f_conv_b"][0], p["f_w_down"][0], p["norm_final"], tm=tm, final_norm=False)
    q, k, v = _inproj_latent(x, p["norm_mix"][1], p["o_w_down"][0], p["o_q_gain"][0], p["o_kv_gain"][0],
                             p["o_w_uq"][0], p["o_w_ukv"][0], cos_l, sin_l, tm=tm)
    o = _flash_attention(q, k, v, tq=min(TQ_MHA, s), tk=min(TK, s // 8))
    x = _mix_ffn(x, [o], [p["o_w_out"][0].astype(BF16)], p["norm_ffn"][1], p["f_w_up"][1], p["f_conv_w"][1],
                 p["f_conv_b"][1], p["f_w_down"][1], p["norm_final"], tm=tm, final_norm=True)
    return x


def kernel(x_prompt, x_sample, norm_mix, norm_ffn, norm_final, e_w_in, e_q_gain, e_k_gain, e_sink, e_w_out, o_w_down, o_q_gain, o_kv_gain, o_w_uq, o_w_ukv, o_w_out, f_w_up, f_conv_w, f_conv_b, f_w_down):
    p = dict(norm_mix=norm_mix, norm_ffn=norm_ffn, norm_final=norm_final, e_w_in=e_w_in, e_q_gain=e_q_gain,
             e_k_gain=e_k_gain, e_sink=e_sink, e_w_out=e_w_out, o_w_down=o_w_down, o_q_gain=o_q_gain,
             o_kv_gain=o_kv_gain, o_w_uq=o_w_uq, o_w_ukv=o_w_ukv, o_w_out=o_w_out, f_w_up=f_w_up,
             f_conv_w=f_conv_w, f_conv_b=f_conv_b, f_w_down=f_w_down)
    tables = _rope_tables(x_prompt.shape[1])
    return _trunk(x_prompt, p, tables), _trunk(x_sample, p, tables)
```

```python
import functools
import math

import numpy as np
import jax
import jax.numpy as jnp
from jax import lax
from jax.experimental import pallas as pl
from jax.experimental.pallas import tpu as pltpu

D_MODEL = 1024
GRID_W = 64
HEAD_DIM = 64
A_HEADS = 8
A_KV_HEADS = 2
B_HEADS = 8
B_KV_HEADS = 2
WINDOW = 128
C_HEADS = 16
Q_LORA = 384
KV_LORA = 256
NOPE_DIM = 64
ROPE_DIM = 32
V_DIM = 64
D_FF = 2816
CONV_W = 3
ROPE_THETA = 10000.0
EPS = 1e-6

LANES = 128
LOG2E = math.log2(math.e)
VMEM_LIMIT = 56 * 1024 * 1024

F32 = jnp.float32
BF16 = jnp.bfloat16

QK_A = (A_HEADS + A_KV_HEADS) * HEAD_DIM


def _params(*sem):
    return pltpu.CompilerParams(dimension_semantics=sem, vmem_limit_bytes=VMEM_LIMIT)


def _const_spec(shape):
    nd = len(shape)
    return pl.BlockSpec(shape, lambda *_: (0,) * nd, pipeline_mode=pl.Buffered(1))


def _rms(x, gain):
    ms = jnp.mean(x * x, axis=-1, keepdims=True)
    return x * lax.rsqrt(ms + EPS) * gain


def _lane_ids(shape):
    return lax.broadcasted_iota(jnp.int32, shape, len(shape) - 1)


def _low_half(x):
    return jnp.where(_lane_ids(x.shape) < HEAD_DIM, x, 0.0)


def _swap_halves(x):
    return pltpu.roll(x, HEAD_DIM, axis=1)


def _with_ones_column(v):
    lane = _lane_ids(v.shape)
    return jnp.where(lane < V_DIM, v, jnp.where(lane == V_DIM, 1.0, 0.0))


def _inproj_even_kernel(x_ref, g_ref, w_ref, avg_ref, gq_ref, gr_ref, cos_ref, sin_ref,
                        qa_ref, ka_ref, va_ref, qb_ref, kb_ref, vb_ref, *, qb_scale):
    h = _rms(x_ref[...], g_ref[...]).astype(BF16)
    y = jnp.dot(h, w_ref[...], preferred_element_type=F32)
    yqk = y[:, :QK_A]
    sq = yqk * yqk
    hi = sq.astype(BF16)
    lo = (sq - hi.astype(F32)).astype(BF16)
    msq = (jnp.dot(hi, avg_ref[...], preferred_element_type=F32)
           + jnp.dot(lo, avg_ref[...], preferred_element_type=F32))
    r = lax.rsqrt(msq + EPS)
    cos = cos_ref[...]
    sin = sin_ref[...]
    rot_half = HEAD_DIM // 4
    first = _lane_ids(cos.shape) % (2 * rot_half) < rot_half
    for c in range(QK_A // LANES):
        sl = slice(c * LANES, (c + 1) * LANES)
        yc = yqk[:, sl]
        swapped = jnp.where(first, pltpu.roll(yc, LANES - rot_half, axis=1), pltpu.roll(yc, rot_half, axis=1))
        a = (yc * (gq_ref[:, sl] * cos) + swapped * (gr_ref[:, sl] * sin)) * r[:, sl]
        even = _low_half(a).astype(BF16)
        odd = _low_half(_swap_halves(a)).astype(BF16)
        if c < A_HEADS // 2:
            qa_ref[2 * c] = even
            qa_ref[2 * c + 1] = odd
        else:
            ka_ref[0] = even
            ka_ref[1] = odd
    base = QK_A
    v = y[:, base:base + LANES]
    va_ref[0] = _with_ones_column(v).astype(BF16)
    va_ref[1] = _with_ones_column(_swap_halves(v)).astype(BF16)
    base += LANES
    for c in range(B_HEADS // 2):
        a = y[:, base + c * LANES: base + (c + 1) * LANES] * qb_scale
        qb_ref[2 * c] = _low_half(a).astype(BF16)
        qb_ref[2 * c + 1] = _low_half(_swap_halves(a)).astype(BF16)
    base += B_HEADS * HEAD_DIM
    kk = y[:, base:base + LANES]
    kb_ref[0] = _low_half(kk).astype(BF16)
    kb_ref[1] = _low_half(_swap_halves(kk)).astype(BF16)
    base += LANES
    v = y[:, base:base + LANES]
    vb_ref[0] = _with_ones_column(v).astype(BF16)
    vb_ref[1] = _with_ones_column(_swap_halves(v)).astype(BF16)


def _rot_half_perm(n_cols, half):
    col = np.arange(n_cols)
    first = (col % (2 * half)) < half
    perm = np.where(first, col + half, col - half)
    sign = np.where(first, -1.0, 1.0).astype(np.float32)
    return perm, sign


def _inproj_even(x, g_mix, w_in, q_gain, k_gain, cos128, sin128, *, tm):
    b, s, d = x.shape
    q_scale = HEAD_DIM ** -0.5 * LOG2E
    perm, sign = _rot_half_perm(QK_A, HEAD_DIM // 4)
    w_all = w_in.astype(BF16)
    gains = jnp.concatenate([jnp.tile(q_gain * q_scale, A_HEADS), jnp.tile(k_gain, A_KV_HEADS)])
    gq = gains.reshape(1, QK_A)
    gr = (gains[perm] * sign).reshape(1, QK_A)
    blk = np.arange(QK_A) // HEAD_DIM
    avg = jnp.asarray((blk[:, None] == blk[None, :]).astype(np.float32) / HEAD_DIM, BF16)
    n_all = w_all.shape[1]

    def heads(n):
        return pl.BlockSpec((None, n, tm, LANES), lambda bi, i: (bi, 0, i, 0))

    def out(n):
        return jax.ShapeDtypeStruct((b, n, s, LANES), BF16)

    return pl.pallas_call(
        functools.partial(_inproj_even_kernel, qb_scale=q_scale),
        grid=(b, s // tm),
        in_specs=[
            pl.BlockSpec((None, tm, d), lambda bi, i: (bi, i, 0)),
            _const_spec((1, d)),
            _const_spec((d, n_all)),
            _const_spec((QK_A, QK_A)),
            _const_spec((1, QK_A)),
            _const_spec((1, QK_A)),
            pl.BlockSpec((tm, LANES), lambda bi, i: (i, 0)),
            pl.BlockSpec((tm, LANES), lambda bi, i: (i, 0)),
        ],
        out_specs=[heads(A_HEADS), heads(A_KV_HEADS), heads(A_KV_HEADS),
                   heads(B_HEADS), heads(B_KV_HEADS), heads(B_KV_HEADS)],
        out_shape=[out(A_HEADS), out(A_KV_HEADS), out(A_KV_HEADS),
                   out(B_HEADS), out(B_KV_HEADS), out(B_KV_HEADS)],
        compiler_params=_params("parallel", "parallel"),
        name="inproj_even",
    )(x, g_mix.reshape(1, d), w_all, avg, gq, gr, cos128, sin128)


def _inproj_latent_kernel(x_ref, g_ref, wd_ref, gq_ref, gkv_ref, wq_ref, wqr_ref, wkv_ref,
                          cos_ref, sin_ref, q_ref, k_ref, v_ref, *, q_scale):
    h = _rms(x_ref[...], g_ref[...]).astype(BF16)
    c = jnp.dot(h, wd_ref[...], preferred_element_type=F32)
    cq = _rms(c[:, :Q_LORA], gq_ref[...]).astype(BF16)
    ckv = _rms(c[:, Q_LORA:Q_LORA + KV_LORA], gkv_ref[...]).astype(BF16)
    base = Q_LORA + KV_LORA
    cos = cos_ref[...]
    sin = sin_ref[...]
    k_rope = c[:, base:base + LANES] * cos + c[:, base + LANES:base + 2 * LANES] * sin
    yq = jnp.dot(cq, wq_ref[...], preferred_element_type=F32)
    yqr = jnp.dot(cq, wqr_ref[...], preferred_element_type=F32)
    ykv = jnp.dot(ckv, wkv_ref[...], preferred_element_type=F32)
    per_tile = LANES // ROPE_DIM
    for hd in range(C_HEADS):
        sl = slice(hd * LANES, (hd + 1) * LANES)
        tile = yqr[:, (hd // per_tile) * LANES:(hd // per_tile + 1) * LANES]
        shift = (NOPE_DIM - (hd % per_tile) * ROPE_DIM) % LANES
        rot = pltpu.roll(tile, shift, axis=1) if shift else tile
        q_ref[hd] = ((yq[:, sl] * cos + rot * sin) * q_scale).astype(BF16)
        kv = ykv[:, sl]
        k_ref[hd] = jnp.where(_lane_ids(kv.shape) < NOPE_DIM, kv, k_rope).astype(BF16)
        v_ref[hd] = _with_ones_column(_swap_halves(kv)).astype(BF16)


def _pad_heads(w, n_heads, width, lo, hi, at):
    k = w.shape[0]
    w3 = w.reshape(k, n_heads, width)[:, :, lo:hi]
    w3 = jnp.pad(w3, ((0, 0), (0, 0), (at, LANES - at - (hi - lo))))
    return w3.reshape(k, n_heads * LANES)


def _inproj_latent(x, g_mix, w_down, q_gain, kv_gain, w_uq, w_ukv, cos128, sin128, *, tm):
    b, s, d = x.shape
    qk_dim = NOPE_DIM + ROPE_DIM
    q_scale = qk_dim ** -0.5 * LOG2E
    perm, sign = _rot_half_perm(ROPE_DIM, ROPE_DIM // 2)
    base = Q_LORA + KV_LORA
    w_kr = w_down[:, base:]
    pad = ((0, 0), (NOPE_DIM, LANES - NOPE_DIM - ROPE_DIM))
    wd = jnp.concatenate([w_down[:, :base], jnp.pad(w_kr, pad), jnp.pad(w_kr[:, perm] * sign, pad)],
                         axis=1).astype(BF16)
    wq = _pad_heads(w_uq, C_HEADS, qk_dim, 0, qk_dim, 0).astype(BF16)
    w_qr = w_uq.reshape(Q_LORA, C_HEADS, qk_dim)[:, :, NOPE_DIM:][:, :, perm] * sign
    wqr = w_qr.reshape(Q_LORA, C_HEADS * ROPE_DIM).astype(BF16)
    wkv = w_ukv.astype(BF16)
    nq = C_HEADS * LANES
    head_spec = pl.BlockSpec((None, C_HEADS, tm, LANES), lambda bi, i: (bi, 0, i, 0))
    out = jax.ShapeDtypeStruct((b, C_HEADS, s, LANES), BF16)
    return pl.pallas_call(
        functools.partial(_inproj_latent_kernel, q_scale=q_scale),
        grid=(b, s // tm),
        in_specs=[
            pl.BlockSpec((None, tm, d), lambda bi, i: (bi, i, 0)),
            _const_spec((1, d)),
            _const_spec(wd.shape),
            _const_spec((1, Q_LORA)),
            _const_spec((1, KV_LORA)),
            _const_spec((Q_LORA, nq)),
            _const_spec(wqr.shape),
            _const_spec(wkv.shape),
            pl.BlockSpec((tm, LANES), lambda bi, i: (i, 0)),
            pl.BlockSpec((tm, LANES), lambda bi, i: (i, 0)),
        ],
        out_specs=[head_spec, head_spec, head_spec],
        out_shape=[out, out, out],
        compiler_params=_params("parallel", "parallel"),
        name="inproj_latent",
    )(x, g_mix.reshape(1, d), wd, q_gain.reshape(1, Q_LORA), kv_gain.reshape(1, KV_LORA),
      wq, wqr, wkv, cos128, sin128)


SCORE_RING = 4
SCORE_AHEAD = 2


def _flash_rows(q, k_ref, v_ref, kv, s_ref, acc_ref, *, tk):
    rows = q.shape[0]
    n_chunks = k_ref.shape[1] // tk
    ring = s_ref.shape[0]
    assert SCORE_AHEAD < ring <= n_chunks
    n_trips = (n_chunks - SCORE_AHEAD) // ring
    acc_ref[...] = jnp.zeros_like(acc_ref)

    def scores(j, slot):
        start = pl.multiple_of(j * tk, tk)
        kc = k_ref[kv, pl.ds(start, tk), :]
        s_ref[slot] = lax.dot_general(q, kc, (((1,), (1,)), ((), ())), preferred_element_type=F32)

    def softmax_pv(j, slot, m):
        start = pl.multiple_of(j * tk, tk)
        vc = v_ref[kv, pl.ds(start, tk), :]
        m_new = jnp.maximum(m, jnp.max(s_ref[slot], axis=1, keepdims=True))
        p = jnp.exp2(s_ref[slot] - m_new).astype(BF16)
        alpha = jnp.exp2(m - m_new)
        acc_ref[...] = acc_ref[...] * alpha + jnp.dot(p, vc, preferred_element_type=F32)
        return m_new

    def body(i, m):
        for r in range(ring):
            scores(i * ring + r + SCORE_AHEAD, (r + SCORE_AHEAD) % ring)
            m = softmax_pv(i * ring + r, r, m)
        return m

    for j in range(SCORE_AHEAD):
        scores(j, j)
    m = lax.fori_loop(0, n_trips, body, jnp.full((rows, 1), -jnp.inf, F32))
    for j in range(n_trips * ring, n_chunks):
        if j + SCORE_AHEAD < n_chunks:
            scores(j + SCORE_AHEAD, (j + SCORE_AHEAD) % ring)
        m = softmax_pv(j, j % ring, m)
    return acc_ref[...]


def _normalise(acc):
    return acc / acc[:, V_DIM:V_DIM + 1]


def _flash_kernel(q_ref, k_ref, v_ref, o_ref, s_ref, acc_ref, *, tq, tk, shared_kv):
    if shared_kv:
        q = q_ref[...].reshape(2 * tq, LANES)
        o = _normalise(_flash_rows(q, k_ref, v_ref, 0, s_ref, acc_ref, tk=tk))
        o0, o1 = o[:tq], o[tq:]
    else:
        o0 = _normalise(_flash_rows(q_ref[0], k_ref, v_ref, 0, s_ref, acc_ref, tk=tk))
        o1 = _normalise(_flash_rows(q_ref[1], k_ref, v_ref, 1, s_ref, acc_ref, tk=tk))
    lane = _lane_ids(o0.shape)
    o_ref[...] = jnp.where(lane < V_DIM, o0, _swap_halves(o1)).astype(o_ref.dtype)


def _flash_attention(q, k, v, *, tq, tk):
    b, h, s, _ = q.shape
    hkv = k.shape[1]
    shared_kv = h // hkv >= 2
    if shared_kv:
        group_pairs = h // hkv // 2
        kv_spec = pl.BlockSpec((None, 1, s, LANES), lambda bi, p, i: (bi, p // group_pairs, 0, 0),
                               pipeline_mode=pl.Buffered(1))
        rows = 2 * tq
    else:
        kv_spec = pl.BlockSpec((None, 2, s, LANES), lambda bi, p, i: (bi, p, 0, 0),
                               pipeline_mode=pl.Buffered(1))
        rows = tq
    return pl.pallas_call(
        functools.partial(_flash_kernel, tq=tq, tk=tk, shared_kv=shared_kv),
        grid=(b, h // 2, s // tq),
        in_specs=[
            pl.BlockSpec((None, 2, tq, LANES), lambda bi, p, i: (bi, p, i, 0)),
            kv_spec,
            kv_spec,
        ],
        out_specs=pl.BlockSpec((None, tq, LANES), lambda bi, p, i: (bi, i, p)),
        out_shape=jax.ShapeDtypeStruct((b, s, h * V_DIM), BF16),
        scratch_shapes=[pltpu.VMEM((SCORE_RING, rows, tk), F32), pltpu.VMEM((rows, LANES), F32)],
        compiler_params=_params("parallel", "parallel", "arbitrary"),
        name="flash_shared" if shared_kv else "flash_mha",
    )(q, k, v)


def _window_kernel(sink_ref, q_ref, kp_ref, km_ref, kn_ref, vp_ref, vm_ref, vn_ref, o_ref,
                   *, tq, seq, group):
    g = pl.program_id(1)
    i = pl.program_id(2)
    kcat = jnp.concatenate([kp_ref[...], km_ref[...], kn_ref[...]], axis=0)
    vcat = jnp.concatenate([vp_ref[...], vm_ref[...], vn_ref[...]], axis=0)
    nk = tq + 2 * WINDOW
    q = q_ref[...].reshape(group * tq, LANES)
    s_all = lax.dot_general(q, kcat, (((1,), (1,)), ((), ())), preferred_element_type=F32)
    row = lax.broadcasted_iota(jnp.int32, (tq, nk), 0)
    col = lax.broadcasted_iota(jnp.int32, (tq, nk), 1)
    dist = jnp.abs(col - WINDOW - row)
    kpos = i * tq - WINDOW + col
    valid = (dist <= WINDOW) & (kpos >= 0) & (kpos < seq)
    dist_f = dist.astype(F32)
    outs = []
    for hh in range(group):
        head = g * group + hh
        slope = jnp.exp2(-8.0 * (head + 1).astype(F32) / B_HEADS) * LOG2E
        sink = sink_ref[head] * LOG2E
        sc = jnp.where(valid, s_all[hh * tq:(hh + 1) * tq] - slope * dist_f, -jnp.inf)
        m = jnp.maximum(jnp.max(sc, axis=1, keepdims=True), sink)
        e = jnp.exp2(sc - m).astype(BF16)
        acc = jnp.dot(e, vcat, preferred_element_type=F32)
        denom = acc[:, V_DIM:V_DIM + 1] + jnp.exp2(sink - m)
        outs.append(acc / denom)
    lane = _lane_ids(outs[0].shape)
    pairs = [jnp.where(lane < V_DIM, outs[2 * c], _swap_halves(outs[2 * c + 1]))
             for c in range(group // 2)]
    o_ref[...] = jnp.concatenate(pairs, axis=1).astype(o_ref.dtype)


def _window_attention(q, k, v, sink, *, tq):
    b, h, s, _ = q.shape
    hkv = k.shape[1]
    group = h // hkv
    per = tq // WINDOW
    last = s // WINDOW - 1

    def main(bi, g, i):
        return (bi, g, i, 0)

    def prev(bi, g, i):
        return (bi, g, jnp.maximum(i * per - 1, 0), 0)

    def nxt(bi, g, i):
        return (bi, g, jnp.minimum((i + 1) * per, last), 0)

    halo = lambda f: pl.BlockSpec((None, None, WINDOW, LANES), f)
    mid = pl.BlockSpec((None, None, tq, LANES), main)
    return pl.pallas_call(
        functools.partial(_window_kernel, tq=tq, seq=s, group=group),
        grid=(b, hkv, s // tq),
        in_specs=[
            pl.BlockSpec(memory_space=pltpu.SMEM),
            pl.BlockSpec((None, group, tq, LANES), main),
            halo(prev), mid, halo(nxt),
            halo(prev), mid, halo(nxt),
        ],
        out_specs=pl.BlockSpec((None, tq, group * V_DIM), lambda bi, g, i: (bi, i, g)),
        out_shape=jax.ShapeDtypeStruct((b, s, h * V_DIM), BF16),
        compiler_params=_params("parallel", "parallel", "parallel"),
        name="window_attention",
    )(sink, q, k, k, k, v, v, v)


HALO = 16


def _mix_ffn_kernel(*refs, tm, n_acts, final_norm):
    x_refs = refs[0:3]
    act_refs = [refs[3 + 3 * a: 6 + 3 * a] for a in range(n_acts)]
    base = 3 + 3 * n_acts
    w_refs = refs[base:base + n_acts]
    g_ref, wg_ref, wv_ref, cw_ref, cb_ref, wd_ref, gf_ref, o_ref = refs[base + n_acts:]
    i = pl.program_id(1)
    ext = tm + 2 * HALO
    x1 = jnp.concatenate([r[...] for r in x_refs], axis=0)
    for triple, w_ref in zip(act_refs, w_refs):
        a = jnp.concatenate([r[...] for r in triple], axis=0)
        x1 = x1 + jnp.dot(a, w_ref[...], preferred_element_type=F32)
    row = lax.broadcasted_iota(jnp.int32, (ext, 1), 0)
    inside = ((row >= HALO) | (i > 0)) & ((row < HALO + tm) | (i < pl.num_programs(1) - 1))
    x1 = jnp.where(inside, x1, 0.0)
    h = _rms(x1, g_ref[...]).astype(BF16)
    mid = slice(HALO, HALO + tm)
    gate = jnp.dot(h, wg_ref[...], preferred_element_type=F32)
    val = jnp.dot(h[mid], wv_ref[...], preferred_element_type=F32)
    g_prev = pltpu.roll(gate, 1, axis=0)[mid]
    g_next = pltpu.roll(gate, ext - 1, axis=0)[mid]
    gc = cb_ref[...] + g_prev * cw_ref[0:1, :] + gate[mid] * cw_ref[1:2, :] + g_next * cw_ref[2:3, :]
    act = (gc * jax.nn.sigmoid(gc) * val).astype(BF16)
    out = x1[mid] + jnp.dot(act, wd_ref[...], preferred_element_type=F32)
    if final_norm:
        out = _rms(out, gf_ref[...])
    o_ref[...] = out


def _mix_ffn(x, acts, w_outs, g_ffn, w_up, conv_w, conv_b, w_down, g_final, *, tm, final_norm):
    b, s, d = x.shape
    per = tm // HALO
    last = s // HALO - 1
    wg = w_up[:, :D_FF].astype(BF16)
    wv = w_up[:, D_FF:].astype(BF16)
    wd = w_down.astype(BF16)

    def triple(width):
        return [
            pl.BlockSpec((None, HALO, width), lambda bi, i: (bi, jnp.maximum(i * per - 1, 0), 0)),
            pl.BlockSpec((None, tm, width), lambda bi, i: (bi, i, 0)),
            pl.BlockSpec((None, HALO, width), lambda bi, i: (bi, jnp.minimum((i + 1) * per, last), 0)),
        ]

    in_specs = triple(d)
    operands = [x, x, x]
    for a in acts:
        in_specs += triple(a.shape[-1])
        operands += [a, a, a]
    consts = list(w_outs) + [g_ffn.reshape(1, d), wg, wv, conv_w, conv_b.reshape(1, D_FF), wd,
                             g_final.reshape(1, d)]
    in_specs += [_const_spec(c.shape) for c in consts]
    return pl.pallas_call(
        functools.partial(_mix_ffn_kernel, tm=tm, n_acts=len(acts), final_norm=final_norm),
        grid=(b, s // tm),
        in_specs=in_specs,
        out_specs=pl.BlockSpec((None, tm, d), lambda bi, i: (bi, i, 0)),
        out_shape=jax.ShapeDtypeStruct((b, s, d), F32),
        compiler_params=_params("parallel", "parallel"),
        name="mix_ffn_final" if final_norm else "mix_ffn",
    )(*operands, *consts)


def _rope_angles(pos, dim):
    inv_freq = ROPE_THETA ** (-jnp.arange(0, dim, 2, dtype=F32) / dim)
    ang = pos.astype(F32)[:, None] * inv_freq[None, :]
    return jnp.cos(ang), jnp.sin(ang)


def _rope_tables(s):
    t = jnp.arange(s)
    n_rows = s // GRID_W
    cos_r, sin_r = (jnp.repeat(a, GRID_W, axis=0) for a in _rope_angles(jnp.arange(n_rows), HEAD_DIM // 2))
    cos_c, sin_c = (jnp.tile(a, (n_rows, 1)) for a in _rope_angles(jnp.arange(GRID_W), HEAD_DIM // 2))
    cos_ax = jnp.tile(jnp.concatenate([cos_r, cos_r, cos_c, cos_c], axis=1), (1, 2))
    sin_ax = jnp.tile(jnp.concatenate([sin_r, sin_r, sin_c, sin_c], axis=1), (1, 2))
    cos_t, sin_t = _rope_angles(t, ROPE_DIM)
    tail = jnp.zeros((s, LANES - NOPE_DIM - ROPE_DIM), F32)
    cos_l = jnp.concatenate([jnp.ones((s, NOPE_DIM), F32), cos_t, cos_t, tail], axis=1)
    sin_l = jnp.concatenate([jnp.zeros((s, NOPE_DIM), F32), sin_t, sin_t, tail], axis=1)
    return cos_ax, sin_ax, cos_l, sin_l


TM = 512
TQ_SHARED = 512
TQ_MHA = 1024
TK = 1024
TQ_WINDOW = 256


def _trunk(x, p, tables):
    cos_ax, sin_ax, cos_l, sin_l = tables
    s = x.shape[1]
    tm = min(TM, s)
    qa, ka, va, qb, kb, vb = _inproj_even(x, p["norm_mix"][0], p["e_w_in"][0], p["e_q_gain"][0],
                                          p["e_k_gain"][0], cos_ax, sin_ax, tm=tm)
    oa = _flash_attention(qa, ka, va, tq=min(TQ_SHARED, s), tk=min(TK, s // 8))
    ob = _window_attention(qb, kb, vb, p["e_sink"][0], tq=min(TQ_WINDOW, s))
    w_out = p["e_w_out"][0].astype(BF16)
    n_a = A_HEADS * HEAD_DIM
    x = _mix_ffn(x, [oa, ob], [w_out[:n_a], w_out[n_a:]], p["norm_ffn"][0], p["f_w_up"][0], p["f_conv_w"][0],
                 p["f_conv_b"][0], p["f_w_down"][0], p["norm_final"], tm=tm, final_norm=False)
    q, k, v = _inproj_latent(x, p["norm_mix"][1], p["o_w_down"][0], p["o_q_gain"][0], p["o_kv_gain"][0],
                             p["o_w_uq"][0], p["o_w_ukv"][0], cos_l, sin_l, tm=tm)
    o = _flash_attention(q, k, v, tq=min(TQ_MHA, s), tk=min(TK, s // 8))
    x = _mix_ffn(x, [o], [p["o_w_out"][0].astype(BF16)], p["norm_ffn"][1], p["f_w_up"][1], p["f_conv_w"][1],
                 p["f_conv_b"][1], p["f_w_down"][1], p["norm_final"], tm=tm, final_norm=True)
    return x


def kernel(x_prompt, x_sample, norm_mix, norm_ffn, norm_final, e_w_in, e_q_gain, e_k_gain, e_sink, e_w_out, o_w_down, o_q_gain, o_kv_gain, o_w_uq, o_w_ukv, o_w_out, f_w_up, f_conv_w, f_conv_b, f_w_down):
    p = dict(norm_mix=norm_mix, norm_ffn=norm_ffn, norm_final=norm_final, e_w_in=e_w_in, e_q_gain=e_q_gain,
             e_k_gain=e_k_gain, e_sink=e_sink, e_w_out=e_w_out, o_w_down=o_w_down, o_q_gain=o_q_gain,
             o_kv_gain=o_kv_gain, o_w_uq=o_w_uq, o_w_ukv=o_w_ukv, o_w_out=o_w_out, f_w_up=f_w_up,
             f_conv_w=f_conv_w, f_conv_b=f_conv_b, f_w_down=f_w_down)
    tables = _rope_tables(x_prompt.shape[1])
    return _trunk(x_prompt, p, tables), _trunk(x_sample, p, tables)
```

```python
import functools
import math

import numpy as np
import jax
import jax.numpy as jnp
from jax import lax
from jax.experimental import pallas as pl
from jax.experimental.pallas import tpu as pltpu

D_MODEL = 1024
GRID_W = 64
HEAD_DIM = 64
A_HEADS = 8
A_KV_HEADS = 2
B_HEADS = 8
B_KV_HEADS = 2
WINDOW = 128
C_HEADS = 16
Q_LORA = 384
KV_LORA = 256
NOPE_DIM = 64
ROPE_DIM = 32
V_DIM = 64
D_FF = 2816
CONV_W = 3
ROPE_THETA = 10000.0
EPS = 1e-6

LANES = 128
LOG2E = math.log2(math.e)
VMEM_LIMIT = 56 * 1024 * 1024

F32 = jnp.float32
BF16 = jnp.bfloat16

QK_A = (A_HEADS + A_KV_HEADS) * HEAD_DIM


def _params(*sem):
    return pltpu.CompilerParams(dimension_semantics=sem, vmem_limit_bytes=VMEM_LIMIT)


def _const_spec(shape):
    nd = len(shape)
    return pl.BlockSpec(shape, lambda *_: (0,) * nd, pipeline_mode=pl.Buffered(1))


def _rms(x, gain):
    ms = jnp.mean(x * x, axis=-1, keepdims=True)
    return x * lax.rsqrt(ms + EPS) * gain


def _lane_ids(shape):
    return lax.broadcasted_iota(jnp.int32, shape, len(shape) - 1)


def _low_half(x):
    return jnp.where(_lane_ids(x.shape) < HEAD_DIM, x, 0.0)


def _swap_halves(x):
    return pltpu.roll(x, HEAD_DIM, axis=1)


def _with_ones_column(v):
    lane = _lane_ids(v.shape)
    return jnp.where(lane < V_DIM, v, jnp.where(lane == V_DIM, 1.0, 0.0))


def _inproj_even_kernel(x_ref, g_ref, w_ref, avg_ref, gq_ref, gr_ref, cos_ref, sin_ref,
                        qa_ref, ka_ref, va_ref, qb_ref, kb_ref, vb_ref, *, qb_scale):
    h = _rms(x_ref[...], g_ref[...]).astype(BF16)
    y = jnp.dot(h, w_ref[...], preferred_element_type=F32)
    yqk = y[:, :QK_A]
    sq = yqk * yqk
    hi = sq.astype(BF16)
    lo = (sq - hi.astype(F32)).astype(BF16)
    msq = (jnp.dot(hi, avg_ref[...], preferred_element_type=F32)
           + jnp.dot(lo, avg_ref[...], preferred_element_type=F32))
    r = lax.rsqrt(msq + EPS)
    cos = cos_ref[...]
    sin = sin_ref[...]
    rot_half = HEAD_DIM // 4
    first = _lane_ids(cos.shape) % (2 * rot_half) < rot_half
    for c in range(QK_A // LANES):
        sl = slice(c * LANES, (c + 1) * LANES)
        yc = yqk[:, sl]
        swapped = jnp.where(first, pltpu.roll(yc, LANES - rot_half, axis=1), pltpu.roll(yc, rot_half, axis=1))
        a = (yc * (gq_ref[:, sl] * cos) + swapped * (gr_ref[:, sl] * sin)) * r[:, sl]
        even = _low_half(a).astype(BF16)
        odd = _low_half(_swap_halves(a)).astype(BF16)
        if c < A_HEADS // 2:
            qa_ref[2 * c] = even
            qa_ref[2 * c + 1] = odd
        else:
            ka_ref[0] = even
            ka_ref[1] = odd
    base = QK_A
    v = y[:, base:base + LANES]
    va_ref[0] = _with_ones_column(v).astype(BF16)
    va_ref[1] = _with_ones_column(_swap_halves(v)).astype(BF16)
    base += LANES
    for c in range(B_HEADS // 2):
        a = y[:, base + c * LANES: base + (c + 1) * LANES] * qb_scale
        qb_ref[2 * c] = _low_half(a).astype(BF16)
        qb_ref[2 * c + 1] = _low_half(_swap_halves(a)).astype(BF16)
    base += B_HEADS * HEAD_DIM
    kk = y[:, base:base + LANES]
    kb_ref[0] = _low_half(kk).astype(BF16)
    kb_ref[1] = _low_half(_swap_halves(kk)).astype(BF16)
    base += LANES
    v = y[:, base:base + LANES]
    vb_ref[0] = _with_ones_column(v).astype(BF16)
    vb_ref[1] = _with_ones_column(_swap_halves(v)).astype(BF16)


def _rot_half_perm(n_cols, half):
    col = np.arange(n_cols)
    first = (col % (2 * half)) < half
    perm = np.where(first, col + half, col - half)
    sign = np.where(first, -1.0, 1.0).astype(np.float32)
    return perm, sign


def _inproj_even(x, g_mix, w_in, q_gain, k_gain, cos128, sin128, *, tm):
    b, s, d = x.shape
    q_scale = HEAD_DIM ** -0.5 * LOG2E
    perm, sign = _rot_half_perm(QK_A, HEAD_DIM // 4)
    w_all = w_in.astype(BF16)
    gains = jnp.concatenate([jnp.tile(q_gain * q_scale, A_HEADS), jnp.tile(k_gain, A_KV_HEADS)])
    gq = gains.reshape(1, QK_A)
    gr = (gains[perm] * sign).reshape(1, QK_A)
    blk = np.arange(QK_A) // HEAD_DIM
    avg = jnp.asarray((blk[:, None] == blk[None, :]).astype(np.float32) / HEAD_DIM, BF16)
    n_all = w_all.shape[1]

    def heads(n):
        return pl.BlockSpec((None, n, tm, LANES), lambda bi, i: (bi, 0, i, 0))

    def out(n):
        return jax.ShapeDtypeStruct((b, n, s, LANES), BF16)

    return pl.pallas_call(
        functools.partial(_inproj_even_kernel, qb_scale=q_scale),
        grid=(b, s // tm),
        in_specs=[
            pl.BlockSpec((None, tm, d), lambda bi, i: (bi, i, 0)),
            _const_spec((1, d)),
            _const_spec((d, n_all)),
            _const_spec((QK_A, QK_A)),
            _const_spec((1, QK_A)),
            _const_spec((1, QK_A)),
            pl.BlockSpec((tm, LANES), lambda bi, i: (i, 0)),
            pl.BlockSpec((tm, LANES), lambda bi, i: (i, 0)),
        ],
        out_specs=[heads(A_HEADS), heads(A_KV_HEADS), heads(A_KV_HEADS),
                   heads(B_HEADS), heads(B_KV_HEADS), heads(B_KV_HEADS)],
        out_shape=[out(A_HEADS), out(A_KV_HEADS), out(A_KV_HEADS),
                   out(B_HEADS), out(B_KV_HEADS), out(B_KV_HEADS)],
        compiler_params=_params("parallel", "parallel"),
        name="inproj_even",
    )(x, g_mix.reshape(1, d), w_all, avg, gq, gr, cos128, sin128)


def _inproj_latent_kernel(x_ref, g_ref, wd_ref, gq_ref, gkv_ref, wq_ref, wqr_ref, wkv_ref,
                          cos_ref, sin_ref, q_ref, k_ref, v_ref, *, q_scale, n_split):
    tm = x_ref.shape[0]
    per_tile = LANES // ROPE_DIM
    base = Q_LORA + KV_LORA
    for part in range(n_split):
        rs = slice(part * (tm // n_split), (part + 1) * (tm // n_split))
        h = _rms(x_ref[rs, :], g_ref[...]).astype(BF16)
        c = jnp.dot(h, wd_ref[...], preferred_element_type=F32)
        cq = _rms(c[:, :Q_LORA], gq_ref[...]).astype(BF16)
        ckv = _rms(c[:, Q_LORA:Q_LORA + KV_LORA], gkv_ref[...]).astype(BF16)
        cos = cos_ref[rs, :]
        sin = sin_ref[rs, :]
        k_rope = c[:, base:base + LANES] * cos + c[:, base + LANES:base + 2 * LANES] * sin
        yq = jnp.dot(cq, wq_ref[...], preferred_element_type=F32)
        yqr = jnp.dot(cq, wqr_ref[...], preferred_element_type=F32)
        ykv = jnp.dot(ckv, wkv_ref[...], preferred_element_type=F32)
        for hd in range(C_HEADS):
            sl = slice(hd * LANES, (hd + 1) * LANES)
            tile = yqr[:, (hd // per_tile) * LANES:(hd // per_tile + 1) * LANES]
            shift = (NOPE_DIM - (hd % per_tile) * ROPE_DIM) % LANES
            rot = pltpu.roll(tile, shift, axis=1) if shift else tile
            q_ref[hd, rs, :] = ((yq[:, sl] * cos + rot * sin) * q_scale).astype(BF16)
            kv = ykv[:, sl]
            k_ref[hd, rs, :] = jnp.where(_lane_ids(kv.shape) < NOPE_DIM, kv, k_rope).astype(BF16)
            v_ref[hd, rs, :] = _with_ones_column(_swap_halves(kv)).astype(BF16)


def _pad_heads(w, n_heads, width, lo, hi, at):
    k = w.shape[0]
    w3 = w.reshape(k, n_heads, width)[:, :, lo:hi]
    w3 = jnp.pad(w3, ((0, 0), (0, 0), (at, LANES - at - (hi - lo))))
    return w3.reshape(k, n_heads * LANES)


def _inproj_latent(x, g_mix, w_down, q_gain, kv_gain, w_uq, w_ukv, cos128, sin128, *, tm):
    b, s, d = x.shape
    qk_dim = NOPE_DIM + ROPE_DIM
    q_scale = qk_dim ** -0.5 * LOG2E
    perm, sign = _rot_half_perm(ROPE_DIM, ROPE_DIM // 2)
    base = Q_LORA + KV_LORA
    w_kr = w_down[:, base:]
    pad = ((0, 0), (NOPE_DIM, LANES - NOPE_DIM - ROPE_DIM))
    wd = jnp.concatenate([w_down[:, :base], jnp.pad(w_kr, pad), jnp.pad(w_kr[:, perm] * sign, pad)],
                         axis=1).astype(BF16)
    wq = _pad_heads(w_uq, C_HEADS, qk_dim, 0, qk_dim, 0).astype(BF16)
    w_qr = w_uq.reshape(Q_LORA, C_HEADS, qk_dim)[:, :, NOPE_DIM:][:, :, perm] * sign
    wqr = w_qr.reshape(Q_LORA, C_HEADS * ROPE_DIM).astype(BF16)
    wkv = w_ukv.astype(BF16)
    nq = C_HEADS * LANES
    head_spec = pl.BlockSpec((None, C_HEADS, tm, LANES), lambda bi, i: (bi, 0, i, 0))
    out = jax.ShapeDtypeStruct((b, C_HEADS, s, LANES), BF16)
    return pl.pallas_call(
        functools.partial(_inproj_latent_kernel, q_scale=q_scale, n_split=2),
        grid=(b, s // tm),
        in_specs=[
            pl.BlockSpec((None, tm, d), lambda bi, i: (bi, i, 0)),
            _const_spec((1, d)),
            _const_spec(wd.shape),
            _const_spec((1, Q_LORA)),
            _const_spec((1, KV_LORA)),
            _const_spec((Q_LORA, nq)),
            _const_spec(wqr.shape),
            _const_spec(wkv.shape),
            pl.BlockSpec((tm, LANES), lambda bi, i: (i, 0)),
            pl.BlockSpec((tm, LANES), lambda bi, i: (i, 0)),
        ],
        out_specs=[head_spec, head_spec, head_spec],
        out_shape=[out, out, out],
        compiler_params=_params("parallel", "parallel"),
        name="inproj_latent",
    )(x, g_mix.reshape(1, d), wd, q_gain.reshape(1, Q_LORA), kv_gain.reshape(1, KV_LORA),
      wq, wqr, wkv, cos128, sin128)


SCORE_RING = 4
SCORE_AHEAD = 2


def _flash_rows(q, k_ref, v_ref, kv, s_ref, acc_ref, *, tk):
    rows = q.shape[0]
    n_chunks = k_ref.shape[1] // tk
    ring = s_ref.shape[0]
    assert SCORE_AHEAD < ring <= n_chunks
    n_trips = (n_chunks - SCORE_AHEAD) // ring
    acc_ref[...] = jnp.zeros_like(acc_ref)

    def scores(j, slot):
        start = pl.multiple_of(j * tk, tk)
        kc = k_ref[kv, pl.ds(start, tk), :]
        s_ref[slot] = lax.dot_general(q, kc, (((1,), (1,)), ((), ())), preferred_element_type=F32)

    def softmax_pv(j, slot, m):
        start = pl.multiple_of(j * tk, tk)
        vc = v_ref[kv, pl.ds(start, tk), :]
        m_new = jnp.maximum(m, jnp.max(s_ref[slot], axis=1, keepdims=True))
        p = jnp.exp2(s_ref[slot] - m_new).astype(BF16)
        alpha = jnp.exp2(m - m_new)
        acc_ref[...] = acc_ref[...] * alpha + jnp.dot(p, vc, preferred_element_type=F32)
        return m_new

    def body(i, m):
        for r in range(ring):
            scores(i * ring + r + SCORE_AHEAD, (r + SCORE_AHEAD) % ring)
            m = softmax_pv(i * ring + r, r, m)
        return m

    for j in range(SCORE_AHEAD):
        scores(j, j)
    m = lax.fori_loop(0, n_trips, body, jnp.full((rows, 1), -jnp.inf, F32))
    for j in range(n_trips * ring, n_chunks):
        if j + SCORE_AHEAD < n_chunks:
            scores(j + SCORE_AHEAD, (j + SCORE_AHEAD) % ring)
        m = softmax_pv(j, j % ring, m)
    return acc_ref[...]


def _normalise(acc):
    return acc / acc[:, V_DIM:V_DIM + 1]


def _flash_kernel(q_ref, k_ref, v_ref, o_ref, s_ref, acc_ref, *, tq, tk, shared_kv):
    if shared_kv:
        q = q_ref[...].reshape(2 * tq, LANES)
        o = _normalise(_flash_rows(q, k_ref, v_ref, 0, s_ref, acc_ref, tk=tk))
        o0, o1 = o[:tq], o[tq:]
    else:
        o0 = _normalise(_flash_rows(q_ref[0], k_ref, v_ref, 0, s_ref, acc_ref, tk=tk))
        o1 = _normalise(_flash_rows(q_ref[1], k_ref, v_ref, 1, s_ref, acc_ref, tk=tk))
    lane = _lane_ids(o0.shape)
    o_ref[...] = jnp.where(lane < V_DIM, o0, _swap_halves(o1)).astype(o_ref.dtype)


def _flash_attention(q, k, v, *, tq, tk):
    b, h, s, _ = q.shape
    hkv = k.shape[1]
    shared_kv = h // hkv >= 2
    if shared_kv:
        group_pairs = h // hkv // 2
        kv_spec = pl.BlockSpec((None, 1, s, LANES), lambda bi, p, i: (bi, p // group_pairs, 0, 0))
        rows = 2 * tq
    else:
        kv_spec = pl.BlockSpec((None, 2, s, LANES), lambda bi, p, i: (bi, p, 0, 0),
                               pipeline_mode=pl.Buffered(1))
        rows = tq
    return pl.pallas_call(
        functools.partial(_flash_kernel, tq=tq, tk=tk, shared_kv=shared_kv),
        grid=(b, h // 2, s // tq),
        in_specs=[
            pl.BlockSpec((None, 2, tq, LANES), lambda bi, p, i: (bi, p, i, 0)),
            kv_spec,
            kv_spec,
        ],
        out_specs=pl.BlockSpec((None, tq, LANES), lambda bi, p, i: (bi, i, p)),
        out_shape=jax.ShapeDtypeStruct((b, s, h * V_DIM), BF16),
        scratch_shapes=[pltpu.VMEM((SCORE_RING, rows, tk), F32), pltpu.VMEM((rows, LANES), F32)],
        compiler_params=_params("parallel", "parallel", "arbitrary"),
        name="flash_shared" if shared_kv else "flash_mha",
    )(q, k, v)


def _window_kernel(sink_ref, q_ref, kp_ref, km_ref, kn_ref, vp_ref, vm_ref, vn_ref, o_ref,
                   *, tq, seq, group):
    i = pl.program_id(1)
    nk = tq + 2 * WINDOW
    row = lax.broadcasted_iota(jnp.int32, (tq, nk), 0)
    col = lax.broadcasted_iota(jnp.int32, (tq, nk), 1)
    dist = jnp.abs(col - WINDOW - row)
    kpos = i * tq - WINDOW + col
    valid = (dist <= WINDOW) & (kpos >= 0) & (kpos < seq)
    dist_f = dist.astype(F32)
    lane = _lane_ids((tq, LANES))
    pairs = []
    for g in range(q_ref.shape[0] // group):
        kcat = jnp.concatenate([kp_ref[g], km_ref[g], kn_ref[g]], axis=0)
        vcat = jnp.concatenate([vp_ref[g], vm_ref[g], vn_ref[g]], axis=0)
        q = q_ref[g * group:(g + 1) * group].reshape(group * tq, LANES)
        s_all = lax.dot_general(q, kcat, (((1,), (1,)), ((), ())), preferred_element_type=F32)
        outs = []
        for hh in range(group):
            head = g * group + hh
            slope = 2.0 ** (-8.0 * (head + 1) / B_HEADS) * LOG2E
            sink = sink_ref[head] * LOG2E
            sc = jnp.where(valid, s_all[hh * tq:(hh + 1) * tq] - slope * dist_f, -jnp.inf)
            m = jnp.maximum(jnp.max(sc, axis=1, keepdims=True), sink)
            e = jnp.exp2(sc - m).astype(BF16)
            acc = jnp.dot(e, vcat, preferred_element_type=F32)
            denom = acc[:, V_DIM:V_DIM + 1] + jnp.exp2(sink - m)
            outs.append(acc / denom)
        pairs += [jnp.where(lane < V_DIM, outs[2 * c], _swap_halves(outs[2 * c + 1]))
                  for c in range(group // 2)]
    o_ref[...] = jnp.concatenate(pairs, axis=1).astype(o_ref.dtype)


def _window_attention(q, k, v, sink, *, tq):
    b, h, s, _ = q.shape
    hkv = k.shape[1]
    per = tq // WINDOW
    last = s // WINDOW - 1

    def main(bi, i):
        return (bi, 0, i, 0)

    def prev(bi, i):
        return (bi, 0, jnp.maximum(i * per - 1, 0), 0)

    def nxt(bi, i):
        return (bi, 0, jnp.minimum((i + 1) * per, last), 0)

    halo = lambda f: pl.BlockSpec((None, hkv, WINDOW, LANES), f)
    mid = pl.BlockSpec((None, hkv, tq, LANES), main)
    return pl.pallas_call(
        functools.partial(_window_kernel, tq=tq, seq=s, group=h // hkv),
        grid=(b, s // tq),
        in_specs=[
            pl.BlockSpec(memory_space=pltpu.SMEM),
            pl.BlockSpec((None, h, tq, LANES), main),
            halo(prev), mid, halo(nxt),
            halo(prev), mid, halo(nxt),
        ],
        out_specs=pl.BlockSpec((None, tq, h * V_DIM), lambda bi, i: (bi, i, 0)),
        out_shape=jax.ShapeDtypeStruct((b, s, h * V_DIM), BF16),
        compiler_params=_params("parallel", "parallel"),
        name="window_attention",
    )(sink, q, k, k, k, v, v, v)


HALO = 16


def _mix_ffn_kernel(*refs, tm, n_acts, final_norm):
    x_refs = refs[0:3]
    act_refs = [refs[3 + 3 * a: 6 + 3 * a] for a in range(n_acts)]
    base = 3 + 3 * n_acts
    w_refs = refs[base:base + n_acts]
    g_ref, wg_ref, wv_ref, cw_ref, cb_ref, wd_ref, gf_ref, o_ref = refs[base + n_acts:]
    i = pl.program_id(1)
    ext = tm + 2 * HALO
    x1 = jnp.concatenate([r[...] for r in x_refs], axis=0)
    for triple, w_ref in zip(act_refs, w_refs):
        a = jnp.concatenate([r[...] for r in triple], axis=0)
        x1 = x1 + jnp.dot(a, w_ref[...], preferred_element_type=F32)
    row = lax.broadcasted_iota(jnp.int32, (ext, 1), 0)
    inside = ((row >= HALO) | (i > 0)) & ((row < HALO + tm) | (i < pl.num_programs(1) - 1))
    x1 = jnp.where(inside, x1, 0.0)
    h = _rms(x1, g_ref[...]).astype(BF16)
    mid = slice(HALO, HALO + tm)
    gate = jnp.dot(h, wg_ref[...], preferred_element_type=F32)
    val = jnp.dot(h[mid], wv_ref[...], preferred_element_type=F32)
    g_prev = pltpu.roll(gate, 1, axis=0)[mid]
    g_next = pltpu.roll(gate, ext - 1, axis=0)[mid]
    gc = cb_ref[...] + g_prev * cw_ref[0:1, :] + gate[mid] * cw_ref[1:2, :] + g_next * cw_ref[2:3, :]
    act = (gc * jax.nn.sigmoid(gc) * val).astype(BF16)
    out = x1[mid] + jnp.dot(act, wd_ref[...], preferred_element_type=F32)
    if final_norm:
        out = _rms(out, gf_ref[...])
    o_ref[...] = out


def _mix_ffn(x, acts, w_outs, g_ffn, w_up, conv_w, conv_b, w_down, g_final, *, tm, final_norm):
    b, s, d = x.shape
    per = tm // HALO
    last = s // HALO - 1
    wg = w_up[:, :D_FF].astype(BF16)
    wv = w_up[:, D_FF:].astype(BF16)
    wd = w_down.astype(BF16)

    def triple(width):
        return [
            pl.BlockSpec((None, HALO, width), lambda bi, i: (bi, jnp.maximum(i * per - 1, 0), 0)),
            pl.BlockSpec((None, tm, width), lambda bi, i: (bi, i, 0)),
            pl.BlockSpec((None, HALO, width), lambda bi, i: (bi, jnp.minimum((i + 1) * per, last), 0)),
        ]

    in_specs = triple(d)
    operands = [x, x, x]
    for a in acts:
        in_specs += triple(a.shape[-1])
        operands += [a, a, a]
    consts = list(w_outs) + [g_ffn.reshape(1, d), wg, wv, conv_w, conv_b.reshape(1, D_FF), wd,
                             g_final.reshape(1, d)]
    in_specs += [_const_spec(c.shape) for c in consts]
    return pl.pallas_call(
        functools.partial(_mix_ffn_kernel, tm=tm, n_acts=len(acts), final_norm=final_norm),
        grid=(b, s // tm),
        in_specs=in_specs,
        out_specs=pl.BlockSpec((None, tm, d), lambda bi, i: (bi, i, 0)),
        out_shape=jax.ShapeDtypeStruct((b, s, d), F32),
        compiler_params=_params("parallel", "parallel"),
        name="mix_ffn_final" if final_norm else "mix_ffn",
    )(*operands, *consts)


def _rope_angles(pos, dim):
    inv_freq = ROPE_THETA ** (-jnp.arange(0, dim, 2, dtype=F32) / dim)
    ang = pos.astype(F32)[:, None] * inv_freq[None, :]
    return jnp.cos(ang), jnp.sin(ang)


def _rope_tables(s):
    t = jnp.arange(s)
    n_rows = s // GRID_W
    cos_r, sin_r = (jnp.repeat(a, GRID_W, axis=0) for a in _rope_angles(jnp.arange(n_rows), HEAD_DIM // 2))
    cos_c, sin_c = (jnp.tile(a, (n_rows, 1)) for a in _rope_angles(jnp.arange(GRID_W), HEAD_DIM // 2))
    cos_ax = jnp.tile(jnp.concatenate([cos_r, cos_r, cos_c, cos_c], axis=1), (1, 2))
    sin_ax = jnp.tile(jnp.concatenate([sin_r, sin_r, sin_c, sin_c], axis=1), (1, 2))
    cos_t, sin_t = _rope_angles(t, ROPE_DIM)
    tail = jnp.zeros((s, LANES - NOPE_DIM - ROPE_DIM), F32)
    cos_l = jnp.concatenate([jnp.ones((s, NOPE_DIM), F32), cos_t, cos_t, tail], axis=1)
    sin_l = jnp.concatenate([jnp.zeros((s, NOPE_DIM), F32), sin_t, sin_t, tail], axis=1)
    return cos_ax, sin_ax, cos_l, sin_l


TM = 512
TQ_SHARED = 512
TQ_MHA = 1024
TK = 1024
TQ_WINDOW = 256


def _trunk(x, p, tables):
    cos_ax, sin_ax, cos_l, sin_l = tables
    s = x.shape[1]
    tm = min(TM, s)
    qa, ka, va, qb, kb, vb = _inproj_even(x, p["norm_mix"][0], p["e_w_in"][0], p["e_q_gain"][0],
                                          p["e_k_gain"][0], cos_ax, sin_ax, tm=tm)
    oa = _flash_attention(qa, ka, va, tq=min(TQ_SHARED, s), tk=min(TK, s // 8))
    ob = _window_attention(qb, kb, vb, p["e_sink"][0], tq=min(TQ_WINDOW, s))
    w_out = p["e_w_out"][0].astype(BF16)
    n_a = A_HEADS * HEAD_DIM
    x = _mix_ffn(x, [oa, ob], [w_out[:n_a], w_out[n_a:]], p["norm_ffn"][0], p["f_w_up"][0], p["f_conv_w"][0],
                 p["f_conv_b"][0], p["f_w_down"][0], p["norm_final"], tm=tm, final_norm=False)
    q, k, v = _inproj_latent(x, p["norm_mix"][1], p["o_w_down"][0], p["o_q_gain"][0], p["o_kv_gain"][0],
                             p["o_w_uq"][0], p["o_w_ukv"][0], cos_l, sin_l, tm=tm)
    o = _flash_attention(q, k, v, tq=min(TQ_MHA, s), tk=min(TK, s // 8))
    x = _mix_ffn(x, [o], [p["o_w_out"][0].astype(BF16)], p["norm_ffn"][1], p["f_w_up"][1], p["f_conv_w"][1],
                 p["f_conv_b"][1], p["f_w_down"][1], p["norm_final"], tm=tm, final_norm=True)
    return x


def kernel(x_prompt, x_sample, norm_mix, norm_ffn, norm_final, e_w_in, e_q_gain, e_k_gain, e_sink, e_w_out, o_w_down, o_q_gain, o_kv_gain, o_w_uq, o_w_ukv, o_w_out, f_w_up, f_conv_w, f_conv_b, f_w_down):
    p = dict(norm_mix=norm_mix, norm_ffn=norm_ffn, norm_final=norm_final, e_w_in=e_w_in, e_q_gain=e_q_gain,
             e_k_gain=e_k_gain, e_sink=e_sink, e_w_out=e_w_out, o_w_down=o_w_down, o_q_gain=o_q_gain,
             o_kv_gain=o_kv_gain, o_w_uq=o_w_uq, o_w_ukv=o_w_ukv, o_w_out=o_w_out, f_w_up=f_w_up,
             f_conv_w=f_conv_w, f_conv_b=f_conv_b, f_w_down=f_w_down)
    tables = _rope_tables(x_prompt.shape[1])
    return _trunk(x_prompt, p, tables), _trunk(x_sample, p, tables)
```

```python
import functools
import math

import numpy as np
import jax
import jax.numpy as jnp
from jax import lax
from jax.experimental import pallas as pl
from jax.experimental.pallas import tpu as pltpu

D_MODEL = 1024
GRID_W = 64
HEAD_DIM = 64
A_HEADS = 8
A_KV_HEADS = 2
B_HEADS = 8
B_KV_HEADS = 2
WINDOW = 128
C_HEADS = 16
Q_LORA = 384
KV_LORA = 256
NOPE_DIM = 64
ROPE_DIM = 32
V_DIM = 64
D_FF = 2816
CONV_W = 3
ROPE_THETA = 10000.0
EPS = 1e-6

LANES = 128
LOG2E = math.log2(math.e)
VMEM_LIMIT = 56 * 1024 * 1024

F32 = jnp.float32
BF16 = jnp.bfloat16

QK_A = (A_HEADS + A_KV_HEADS) * HEAD_DIM


def _params(*sem):
    return pltpu.CompilerParams(dimension_semantics=sem, vmem_limit_bytes=VMEM_LIMIT)


def _const_spec(shape):
    nd = len(shape)
    return pl.BlockSpec(shape, lambda *_: (0,) * nd, pipeline_mode=pl.Buffered(1))


def _rms(x, gain):
    ms = jnp.mean(x * x, axis=-1, keepdims=True)
    return x * lax.rsqrt(ms + EPS) * gain


def _lane_ids(shape):
    return lax.broadcasted_iota(jnp.int32, shape, len(shape) - 1)


def _low_half(x):
    return jnp.where(_lane_ids(x.shape) < HEAD_DIM, x, 0.0)


def _swap_halves(x):
    return pltpu.roll(x, HEAD_DIM, axis=1)


def _with_ones_column(v):
    lane = _lane_ids(v.shape)
    return jnp.where(lane < V_DIM, v, jnp.where(lane == V_DIM, 1.0, 0.0))


def _inproj_even_kernel(x_ref, g_ref, w_ref, avg_ref, gq_ref, gr_ref, cos_ref, sin_ref,
                        qa_ref, ka_ref, va_ref, qb_ref, kb_ref, vb_ref, *, qb_scale):
    h = _rms(x_ref[...], g_ref[...]).astype(BF16)
    y = jnp.dot(h, w_ref[...], preferred_element_type=F32)
    yqk = y[:, :QK_A]
    sq = yqk * yqk
    hi = sq.astype(BF16)
    lo = (sq - hi.astype(F32)).astype(BF16)
    msq = (jnp.dot(hi, avg_ref[...], preferred_element_type=F32)
           + jnp.dot(lo, avg_ref[...], preferred_element_type=F32))
    r = lax.rsqrt(msq + EPS)
    cos = cos_ref[...]
    sin = sin_ref[...]
    rot_half = HEAD_DIM // 4
    first = _lane_ids(cos.shape) % (2 * rot_half) < rot_half
    for c in range(QK_A // LANES):
        sl = slice(c * LANES, (c + 1) * LANES)
        yc = yqk[:, sl]
        swapped = jnp.where(first, pltpu.roll(yc, LANES - rot_half, axis=1), pltpu.roll(yc, rot_half, axis=1))
        a = (yc * (gq_ref[:, sl] * cos) + swapped * (gr_ref[:, sl] * sin)) * r[:, sl]
        even = _low_half(a).astype(BF16)
        odd = _low_half(_swap_halves(a)).astype(BF16)
        if c < A_HEADS // 2:
            qa_ref[2 * c] = even
            qa_ref[2 * c + 1] = odd
        else:
            ka_ref[0] = even
            ka_ref[1] = odd
    base = QK_A
    v = y[:, base:base + LANES]
    va_ref[0] = _with_ones_column(v).astype(BF16)
    va_ref[1] = _with_ones_column(_swap_halves(v)).astype(BF16)
    base += LANES
    for c in range(B_HEADS // 2):
        a = y[:, base + c * LANES: base + (c + 1) * LANES] * qb_scale
        qb_ref[2 * c] = _low_half(a).astype(BF16)
        qb_ref[2 * c + 1] = _low_half(_swap_halves(a)).astype(BF16)
    base += B_HEADS * HEAD_DIM
    kk = y[:, base:base + LANES]
    kb_ref[0] = _low_half(kk).astype(BF16)
    kb_ref[1] = _low_half(_swap_halves(kk)).astype(BF16)
    base += LANES
    v = y[:, base:base + LANES]
    vb_ref[0] = _with_ones_column(v).astype(BF16)
    vb_ref[1] = _with_ones_column(_swap_halves(v)).astype(BF16)


def _rot_half_perm(n_cols, half):
    col = np.arange(n_cols)
    first = (col % (2 * half)) < half
    perm = np.where(first, col + half, col - half)
    sign = np.where(first, -1.0, 1.0).astype(np.float32)
    return perm, sign


def _inproj_even(x, g_mix, w_in, q_gain, k_gain, cos128, sin128, *, tm):
    b, s, d = x.shape
    q_scale = HEAD_DIM ** -0.5 * LOG2E
    perm, sign = _rot_half_perm(QK_A, HEAD_DIM // 4)
    w_all = w_in.astype(BF16)
    gains = jnp.concatenate([jnp.tile(q_gain * q_scale, A_HEADS), jnp.tile(k_gain, A_KV_HEADS)])
    gq = gains.reshape(1, QK_A)
    gr = (gains[perm] * sign).reshape(1, QK_A)
    blk = np.arange(QK_A) // HEAD_DIM
    avg = jnp.asarray((blk[:, None] == blk[None, :]).astype(np.float32) / HEAD_DIM, BF16)
    n_all = w_all.shape[1]

    def heads(n):
        return pl.BlockSpec((None, n, tm, LANES), lambda bi, i: (bi, 0, i, 0))

    def out(n):
        return jax.ShapeDtypeStruct((b, n, s, LANES), BF16)

    return pl.pallas_call(
        functools.partial(_inproj_even_kernel, qb_scale=q_scale),
        grid=(b, s // tm),
        in_specs=[
            pl.BlockSpec((None, tm, d), lambda bi, i: (bi, i, 0)),
            _const_spec((1, d)),
            _const_spec((d, n_all)),
            _const_spec((QK_A, QK_A)),
            _const_spec((1, QK_A)),
            _const_spec((1, QK_A)),
            pl.BlockSpec((tm, LANES), lambda bi, i: (i, 0)),
            pl.BlockSpec((tm, LANES), lambda bi, i: (i, 0)),
        ],
        out_specs=[heads(A_HEADS), heads(A_KV_HEADS), heads(A_KV_HEADS),
                   heads(B_HEADS), heads(B_KV_HEADS), heads(B_KV_HEADS)],
        out_shape=[out(A_HEADS), out(A_KV_HEADS), out(A_KV_HEADS),
                   out(B_HEADS), out(B_KV_HEADS), out(B_KV_HEADS)],
        compiler_params=_params("parallel", "parallel"),
        name="inproj_even",
    )(x, g_mix.reshape(1, d), w_all, avg, gq, gr, cos128, sin128)


def _inproj_latent_kernel(x_ref, g_ref, wd_ref, gq_ref, gkv_ref, wq_ref, wqr_ref, wkv_ref,
                          cos_ref, sin_ref, q_ref, k_ref, v_ref, *, q_scale, n_split):
    tm = x_ref.shape[0]
    per_tile = LANES // ROPE_DIM
    base = Q_LORA + KV_LORA
    for part in range(n_split):
        rs = slice(part * (tm // n_split), (part + 1) * (tm // n_split))
        h = _rms(x_ref[rs, :], g_ref[...]).astype(BF16)
        c = jnp.dot(h, wd_ref[...], preferred_element_type=F32)
        cq = _rms(c[:, :Q_LORA], gq_ref[...]).astype(BF16)
        ckv = _rms(c[:, Q_LORA:Q_LORA + KV_LORA], gkv_ref[...]).astype(BF16)
        cos = cos_ref[rs, :]
        sin = sin_ref[rs, :]
        k_rope = c[:, base:base + LANES] * cos + c[:, base + LANES:base + 2 * LANES] * sin
        yq = jnp.dot(cq, wq_ref[...], preferred_element_type=F32)
        yqr = jnp.dot(cq, wqr_ref[...], preferred_element_type=F32)
        ykv = jnp.dot(ckv, wkv_ref[...], preferred_element_type=F32)
        for hd in range(C_HEADS):
            sl = slice(hd * LANES, (hd + 1) * LANES)
            tile = yqr[:, (hd // per_tile) * LANES:(hd // per_tile + 1) * LANES]
            shift = (NOPE_DIM - (hd % per_tile) * ROPE_DIM) % LANES
            rot = pltpu.roll(tile, shift, axis=1) if shift else tile
            q_ref[hd, rs, :] = ((yq[:, sl] * cos + rot * sin) * q_scale).astype(BF16)
            kv = ykv[:, sl]
            k_ref[hd, rs, :] = jnp.where(_lane_ids(kv.shape) < NOPE_DIM, kv, k_rope).astype(BF16)
            v_ref[hd, rs, :] = _with_ones_column(_swap_halves(kv)).astype(BF16)


def _pad_heads(w, n_heads, width, lo, hi, at):
    k = w.shape[0]
    w3 = w.reshape(k, n_heads, width)[:, :, lo:hi]
    w3 = jnp.pad(w3, ((0, 0), (0, 0), (at, LANES - at - (hi - lo))))
    return w3.reshape(k, n_heads * LANES)


def _inproj_latent(x, g_mix, w_down, q_gain, kv_gain, w_uq, w_ukv, cos128, sin128, *, tm):
    b, s, d = x.shape
    qk_dim = NOPE_DIM + ROPE_DIM
    q_scale = qk_dim ** -0.5 * LOG2E
    perm, sign = _rot_half_perm(ROPE_DIM, ROPE_DIM // 2)
    base = Q_LORA + KV_LORA
    w_kr = w_down[:, base:]
    pad = ((0, 0), (NOPE_DIM, LANES - NOPE_DIM - ROPE_DIM))
    wd = jnp.concatenate([w_down[:, :base], jnp.pad(w_kr, pad), jnp.pad(w_kr[:, perm] * sign, pad)],
                         axis=1).astype(BF16)
    wq = _pad_heads(w_uq, C_HEADS, qk_dim, 0, qk_dim, 0).astype(BF16)
    w_qr = w_uq.reshape(Q_LORA, C_HEADS, qk_dim)[:, :, NOPE_DIM:][:, :, perm] * sign
    wqr = w_qr.reshape(Q_LORA, C_HEADS * ROPE_DIM).astype(BF16)
    wkv = w_ukv.astype(BF16)
    nq = C_HEADS * LANES
    head_spec = pl.BlockSpec((None, C_HEADS, tm, LANES), lambda bi, i: (bi, 0, i, 0))
    out = jax.ShapeDtypeStruct((b, C_HEADS, s, LANES), BF16)
    return pl.pallas_call(
        functools.partial(_inproj_latent_kernel, q_scale=q_scale, n_split=2),
        grid=(b, s // tm),
        in_specs=[
            pl.BlockSpec((None, tm, d), lambda bi, i: (bi, i, 0)),
            _const_spec((1, d)),
            _const_spec(wd.shape),
            _const_spec((1, Q_LORA)),
            _const_spec((1, KV_LORA)),
            _const_spec((Q_LORA, nq)),
            _const_spec(wqr.shape),
            _const_spec(wkv.shape),
            pl.BlockSpec((tm, LANES), lambda bi, i: (i, 0)),
            pl.BlockSpec((tm, LANES), lambda bi, i: (i, 0)),
        ],
        out_specs=[head_spec, head_spec, head_spec],
        out_shape=[out, out, out],
        compiler_params=_params("parallel", "parallel"),
        name="inproj_latent",
    )(x, g_mix.reshape(1, d), wd, q_gain.reshape(1, Q_LORA), kv_gain.reshape(1, KV_LORA),
      wq, wqr, wkv, cos128, sin128)


SCORE_RING = 4
SCORE_AHEAD = 2


def _flash_rows(q, k_ref, v_ref, kv, s_ref, acc_ref, *, tk):
    rows = q.shape[0]
    n_chunks = k_ref.shape[1] // tk
    ring = s_ref.shape[0]
    assert SCORE_AHEAD < ring <= n_chunks
    n_trips = (n_chunks - SCORE_AHEAD) // ring
    acc_ref[...] = jnp.zeros_like(acc_ref)

    def scores(j, slot):
        start = pl.multiple_of(j * tk, tk)
        kc = k_ref[kv, pl.ds(start, tk), :]
        s_ref[slot, :, :tk] = lax.dot_general(q, kc, (((1,), (1,)), ((), ())), preferred_element_type=F32)

    def softmax_pv(j, slot, m):
        start = pl.multiple_of(j * tk, tk)
        vc = v_ref[kv, pl.ds(start, tk), :]
        m_new = jnp.maximum(m, jnp.max(s_ref[slot, :, :tk], axis=1, keepdims=True))
        p = jnp.exp2(s_ref[slot, :, :tk] - m_new).astype(BF16)
        alpha = jnp.exp2(m - m_new)
        acc_ref[...] = acc_ref[...] * alpha + jnp.dot(p, vc, preferred_element_type=F32)
        return m_new

    def body(i, m):
        for r in range(ring):
            scores(i * ring + r + SCORE_AHEAD, (r + SCORE_AHEAD) % ring)
            m = softmax_pv(i * ring + r, r, m)
        return m

    for j in range(SCORE_AHEAD):
        scores(j, j)
    m = lax.fori_loop(0, n_trips, body, jnp.full((rows, 1), -jnp.inf, F32))
    for j in range(n_trips * ring, n_chunks):
        if j + SCORE_AHEAD < n_chunks:
            scores(j + SCORE_AHEAD, (j + SCORE_AHEAD) % ring)
        m = softmax_pv(j, j % ring, m)
    return acc_ref[...]


def _normalise(acc):
    return acc / acc[:, V_DIM:V_DIM + 1]


def _flash_kernel(q_ref, k_ref, v_ref, o_ref, s_ref, acc_ref, *, tq, tk, shared_kv):
    if shared_kv:
        q = q_ref[...].reshape(2 * tq, LANES)
        o = _normalise(_flash_rows(q, k_ref, v_ref, 0, s_ref, acc_ref, tk=tk))
        o0, o1 = o[:tq], o[tq:]
    else:
        o0 = _normalise(_flash_rows(q_ref[0], k_ref, v_ref, 0, s_ref, acc_ref, tk=tk))
        o1 = _normalise(_flash_rows(q_ref[1], k_ref, v_ref, 1, s_ref, acc_ref, tk=tk))
    lane = _lane_ids(o0.shape)
    o_ref[...] = jnp.where(lane < V_DIM, o0, _swap_halves(o1)).astype(o_ref.dtype)


def _flash_attention(q, k, v, *, tq, tk):
    b, h, s, _ = q.shape
    hkv = k.shape[1]
    shared_kv = h // hkv >= 2
    if shared_kv:
        group_pairs = h // hkv // 2
        kv_spec = pl.BlockSpec((None, 1, s, LANES), lambda bi, p, i: (bi, p // group_pairs, 0, 0))
        rows = 2 * tq
    else:
        kv_spec = pl.BlockSpec((None, 2, s, LANES), lambda bi, p, i: (bi, p, 0, 0),
                               pipeline_mode=pl.Buffered(1))
        rows = tq
    return pl.pallas_call(
        functools.partial(_flash_kernel, tq=tq, tk=tk, shared_kv=shared_kv),
        grid=(b, h // 2, s // tq),
        in_specs=[
            pl.BlockSpec((None, 2, tq, LANES), lambda bi, p, i: (bi, p, i, 0)),
            kv_spec,
            kv_spec,
        ],
        out_specs=pl.BlockSpec((None, tq, LANES), lambda bi, p, i: (bi, i, p)),
        out_shape=jax.ShapeDtypeStruct((b, s, h * V_DIM), BF16),
        scratch_shapes=[pltpu.VMEM((SCORE_RING, rows, tk + LANES), F32), pltpu.VMEM((rows, LANES), F32)],
        compiler_params=_params("parallel", "parallel", "arbitrary"),
        name="flash_shared" if shared_kv else "flash_mha",
    )(q, k, v)


def _window_kernel(sink_ref, q_ref, kp_ref, km_ref, kn_ref, vp_ref, vm_ref, vn_ref, o_ref,
                   *, tq, seq, group):
    i = pl.program_id(1)
    nk = tq + 2 * WINDOW
    row = lax.broadcasted_iota(jnp.int32, (tq, nk), 0)
    col = lax.broadcasted_iota(jnp.int32, (tq, nk), 1)
    dist = jnp.abs(col - WINDOW - row)
    kpos = i * tq - WINDOW + col
    valid = (dist <= WINDOW) & (kpos >= 0) & (kpos < seq)
    dist_f = dist.astype(F32)
    lane = _lane_ids((tq, LANES))
    pairs = []
    for g in range(q_ref.shape[0] // group):
        kcat = jnp.concatenate([kp_ref[g], km_ref[g], kn_ref[g]], axis=0)
        vcat = jnp.concatenate([vp_ref[g], vm_ref[g], vn_ref[g]], axis=0)
        q = q_ref[g * group:(g + 1) * group].reshape(group * tq, LANES)
        s_all = lax.dot_general(q, kcat, (((1,), (1,)), ((), ())), preferred_element_type=F32)
        outs = []
        for hh in range(group):
            head = g * group + hh
            slope = 2.0 ** (-8.0 * (head + 1) / B_HEADS) * LOG2E
            sink = sink_ref[head] * LOG2E
            sc = jnp.where(valid, s_all[hh * tq:(hh + 1) * tq] - slope * dist_f, -jnp.inf)
            m = jnp.maximum(jnp.max(sc, axis=1, keepdims=True), sink)
            e = jnp.exp2(sc - m).astype(BF16)
            acc = jnp.dot(e, vcat, preferred_element_type=F32)
            denom = acc[:, V_DIM:V_DIM + 1] + jnp.exp2(sink - m)
            outs.append(acc / denom)
        pairs += [jnp.where(lane < V_DIM, outs[2 * c], _swap_halves(outs[2 * c + 1]))
                  for c in range(group // 2)]
    o_ref[...] = jnp.concatenate(pairs, axis=1).astype(o_ref.dtype)


def _window_attention(q, k, v, sink, *, tq):
    b, h, s, _ = q.shape
    hkv = k.shape[1]
    per = tq // WINDOW
    last = s // WINDOW - 1

    def main(bi, i):
        return (bi, 0, i, 0)

    def prev(bi, i):
        return (bi, 0, jnp.maximum(i * per - 1, 0), 0)

    def nxt(bi, i):
        return (bi, 0, jnp.minimum((i + 1) * per, last), 0)

    halo = lambda f: pl.BlockSpec((None, hkv, WINDOW, LANES), f)
    mid = pl.BlockSpec((None, hkv, tq, LANES), main)
    return pl.pallas_call(
        functools.partial(_window_kernel, tq=tq, seq=s, group=h // hkv),
        grid=(b, s // tq),
        in_specs=[
            pl.BlockSpec(memory_space=pltpu.SMEM),
            pl.BlockSpec((None, h, tq, LANES), main),
            halo(prev), mid, halo(nxt),
            halo(prev), mid, halo(nxt),
        ],
        out_specs=pl.BlockSpec((None, tq, h * V_DIM), lambda bi, i: (bi, i, 0)),
        out_shape=jax.ShapeDtypeStruct((b, s, h * V_DIM), BF16),
        compiler_params=_params("parallel", "parallel"),
        name="window_attention",
    )(sink, q, k, k, k, v, v, v)


HALO = 16


def _mix_ffn_kernel(*refs, tm, n_acts, final_norm):
    x_refs = refs[0:3]
    act_refs = [refs[3 + 3 * a: 6 + 3 * a] for a in range(n_acts)]
    base = 3 + 3 * n_acts
    w_refs = refs[base:base + n_acts]
    g_ref, wg_ref, wv_ref, cw_ref, cb_ref, wd_ref, gf_ref, o_ref = refs[base + n_acts:]
    i = pl.program_id(1)
    ext = tm + 2 * HALO
    x1 = jnp.concatenate([r[...] for r in x_refs], axis=0)
    for triple, w_ref in zip(act_refs, w_refs):
        a = jnp.concatenate([r[...] for r in triple], axis=0)
        x1 = x1 + jnp.dot(a, w_ref[...], preferred_element_type=F32)
    row = lax.broadcasted_iota(jnp.int32, (ext, 1), 0)
    inside = ((row >= HALO) | (i > 0)) & ((row < HALO + tm) | (i < pl.num_programs(1) - 1))
    x1 = jnp.where(inside, x1, 0.0)
    h = _rms(x1, g_ref[...]).astype(BF16)
    mid = slice(HALO, HALO + tm)
    gate = jnp.dot(h, wg_ref[...], preferred_element_type=F32)
    val = jnp.dot(h[mid], wv_ref[...], preferred_element_type=F32)
    g_prev = pltpu.roll(gate, 1, axis=0)[mid]
    g_next = pltpu.roll(gate, ext - 1, axis=0)[mid]
    gc = cb_ref[...] + g_prev * cw_ref[0:1, :] + gate[mid] * cw_ref[1:2, :] + g_next * cw_ref[2:3, :]
    act = (gc * jax.nn.sigmoid(gc) * val).astype(BF16)
    out = x1[mid] + jnp.dot(act, wd_ref[...], preferred_element_type=F32)
    if final_norm:
        out = _rms(out, gf_ref[...])
    o_ref[...] = out


def _mix_ffn(x, acts, w_outs, g_ffn, w_up, conv_w, conv_b, w_down, g_final, *, tm, final_norm):
    b, s, d = x.shape
    per = tm // HALO
    last = s // HALO - 1
    wg = w_up[:, :D_FF].astype(BF16)
    wv = w_up[:, D_FF:].astype(BF16)
    wd = w_down.astype(BF16)

    def triple(width):
        return [
            pl.BlockSpec((None, HALO, width), lambda bi, i: (bi, jnp.maximum(i * per - 1, 0), 0)),
            pl.BlockSpec((None, tm, width), lambda bi, i: (bi, i, 0)),
            pl.BlockSpec((None, HALO, width), lambda bi, i: (bi, jnp.minimum((i + 1) * per, last), 0)),
        ]

    in_specs = triple(d)
    operands = [x, x, x]
    for a in acts:
        in_specs += triple(a.shape[-1])
        operands += [a, a, a]
    consts = list(w_outs) + [g_ffn.reshape(1, d), wg, wv, conv_w, conv_b.reshape(1, D_FF), wd,
                             g_final.reshape(1, d)]
    in_specs += [_const_spec(c.shape) for c in consts]
    return pl.pallas_call(
        functools.partial(_mix_ffn_kernel, tm=tm, n_acts=len(acts), final_norm=final_norm),
        grid=(b, s // tm),
        in_specs=in_specs,
        out_specs=pl.BlockSpec((None, tm, d), lambda bi, i: (bi, i, 0)),
        out_shape=jax.ShapeDtypeStruct((b, s, d), F32),
        compiler_params=_params("parallel", "parallel"),
        name="mix_ffn_final" if final_norm else "mix_ffn",
    )(*operands, *consts)


def _rope_angles(pos, dim):
    inv_freq = ROPE_THETA ** (-jnp.arange(0, dim, 2, dtype=F32) / dim)
    ang = pos.astype(F32)[:, None] * inv_freq[None, :]
    return jnp.cos(ang), jnp.sin(ang)


def _rope_tables(s):
    t = jnp.arange(s)
    n_rows = s // GRID_W
    cos_r, sin_r = (jnp.repeat(a, GRID_W, axis=0) for a in _rope_angles(jnp.arange(n_rows), HEAD_DIM // 2))
    cos_c, sin_c = (jnp.tile(a, (n_rows, 1)) for a in _rope_angles(jnp.arange(GRID_W), HEAD_DIM // 2))
    cos_ax = jnp.tile(jnp.concatenate([cos_r, cos_r, cos_c, cos_c], axis=1), (1, 2))
    sin_ax = jnp.tile(jnp.concatenate([sin_r, sin_r, sin_c, sin_c], axis=1), (1, 2))
    cos_t, sin_t = _rope_angles(t, ROPE_DIM)
    tail = jnp.zeros((s, LANES - NOPE_DIM - ROPE_DIM), F32)
    cos_l = jnp.concatenate([jnp.ones((s, NOPE_DIM), F32), cos_t, cos_t, tail], axis=1)
    sin_l = jnp.concatenate([jnp.zeros((s, NOPE_DIM), F32), sin_t, sin_t, tail], axis=1)
    return cos_ax, sin_ax, cos_l, sin_l


TM = 512
TQ_SHARED = 512
TQ_MHA = 1024
TK = 1024
TQ_WINDOW = 256


def _trunk(x, p, tables):
    cos_ax, sin_ax, cos_l, sin_l = tables
    s = x.shape[1]
    tm = min(TM, s)
    qa, ka, va, qb, kb, vb = _inproj_even(x, p["norm_mix"][0], p["e_w_in"][0], p["e_q_gain"][0],
                                          p["e_k_gain"][0], cos_ax, sin_ax, tm=tm)
    oa = _flash_attention(qa, ka, va, tq=min(TQ_SHARED, s), tk=min(TK, s // 8))
    ob = _window_attention(qb, kb, vb, p["e_sink"][0], tq=min(TQ_WINDOW, s))
    w_out = p["e_w_out"][0].astype(BF16)
    n_a = A_HEADS * HEAD_DIM
    x = _mix_ffn(x, [oa, ob], [w_out[:n_a], w_out[n_a:]], p["norm_ffn"][0], p["f_w_up"][0], p["f_conv_w"][0],
                 p["f_conv_b"][0], p["f_w_down"][0], p["norm_final"], tm=tm, final_norm=False)
    q, k, v = _inproj_latent(x, p["norm_mix"][1], p["o_w_down"][0], p["o_q_gain"][0], p["o_kv_gain"][0],
                             p["o_w_uq"][0], p["o_w_ukv"][0], cos_l, sin_l, tm=tm)
    o = _flash_attention(q, k, v, tq=min(TQ_MHA, s), tk=min(TK, s // 8))
    x = _mix_ffn(x, [o], [p["o_w_out"][0].astype(BF16)], p["norm_ffn"][1], p["f_w_up"][1], p["f_conv_w"][1],
                 p["f_conv_b"][1], p["f_w_down"][1], p["norm_final"], tm=tm, final_norm=True)
    return x


def kernel(x_prompt, x_sample, norm_mix, norm_ffn, norm_final, e_w_in, e_q_gain, e_k_gain, e_sink, e_w_out, o_w_down, o_q_gain, o_kv_gain, o_w_uq, o_w_ukv, o_w_out, f_w_up, f_conv_w, f_conv_b, f_w_down):
    p = dict(norm_mix=norm_mix, norm_ffn=norm_ffn, norm_final=norm_final, e_w_in=e_w_in, e_q_gain=e_q_gain,
             e_k_gain=e_k_gain, e_sink=e_sink, e_w_out=e_w_out, o_w_down=o_w_down, o_q_gain=o_q_gain,
             o_kv_gain=o_kv_gain, o_w_uq=o_w_uq, o_w_ukv=o_w_ukv, o_w_out=o_w_out, f_w_up=f_w_up,
             f_conv_w=f_conv_w, f_conv_b=f_conv_b, f_w_down=f_w_down)
    tables = _rope_tables(x_prompt.shape[1])
    return _trunk(x_prompt, p, tables), _trunk(x_sample, p, tables)
```

```python
import functools
import math

import numpy as np
import jax
import jax.numpy as jnp
from jax import lax
from jax.experimental import pallas as pl
from jax.experimental.pallas import tpu as pltpu

D_MODEL = 1024
GRID_W = 64
HEAD_DIM = 64
A_HEADS = 8
A_KV_HEADS = 2
B_HEADS = 8
B_KV_HEADS = 2
WINDOW = 128
C_HEADS = 16
Q_LORA = 384
KV_LORA = 256
NOPE_DIM = 64
ROPE_DIM = 32
V_DIM = 64
D_FF = 2816
CONV_W = 3
ROPE_THETA = 10000.0
EPS = 1e-6

LANES = 128
LOG2E = math.log2(math.e)
VMEM_LIMIT = 56 * 1024 * 1024

F32 = jnp.float32
BF16 = jnp.bfloat16

QK_A = (A_HEADS + A_KV_HEADS) * HEAD_DIM


def _params(*sem):
    return pltpu.CompilerParams(dimension_semantics=sem, vmem_limit_bytes=VMEM_LIMIT)


def _const_spec(shape):
    nd = len(shape)
    return pl.BlockSpec(shape, lambda *_: (0,) * nd, pipeline_mode=pl.Buffered(1))


def _rms(x, gain):
    ms = jnp.mean(x * x, axis=-1, keepdims=True)
    return x * lax.rsqrt(ms + EPS) * gain


def _lane_ids(shape):
    return lax.broadcasted_iota(jnp.int32, shape, len(shape) - 1)


def _low_half(x):
    return jnp.where(_lane_ids(x.shape) < HEAD_DIM, x, 0.0)


def _swap_halves(x):
    return pltpu.roll(x, HEAD_DIM, axis=1)


def _with_ones_column(v):
    lane = _lane_ids(v.shape)
    return jnp.where(lane < V_DIM, v, jnp.where(lane == V_DIM, 1.0, 0.0))


def _inproj_even_kernel(x_ref, g_ref, w_ref, avg_ref, gq_ref, gr_ref, cos_ref, sin_ref,
                        qa_ref, ka_ref, va_ref, qb_ref, kb_ref, vb_ref, *, qb_scale):
    h = _rms(x_ref[...], g_ref[...]).astype(BF16)
    y = jnp.dot(h, w_ref[...], preferred_element_type=F32)
    yqk = y[:, :QK_A]
    sq = yqk * yqk
    hi = sq.astype(BF16)
    lo = (sq - hi.astype(F32)).astype(BF16)
    msq = (jnp.dot(hi, avg_ref[...], preferred_element_type=F32)
           + jnp.dot(lo, avg_ref[...], preferred_element_type=F32))
    r = lax.rsqrt(msq + EPS)
    cos = cos_ref[...]
    sin = sin_ref[...]
    rot_half = HEAD_DIM // 4
    first = _lane_ids(cos.shape) % (2 * rot_half) < rot_half
    for c in range(QK_A // LANES):
        sl = slice(c * LANES, (c + 1) * LANES)
        yc = yqk[:, sl]
        swapped = jnp.where(first, pltpu.roll(yc, LANES - rot_half, axis=1), pltpu.roll(yc, rot_half, axis=1))
        a = (yc * (gq_ref[:, sl] * cos) + swapped * (gr_ref[:, sl] * sin)) * r[:, sl]
        even = _low_half(a).astype(BF16)
        odd = _low_half(_swap_halves(a)).astype(BF16)
        if c < A_HEADS // 2:
            qa_ref[2 * c] = even
            qa_ref[2 * c + 1] = odd
        else:
            ka_ref[0] = even
            ka_ref[1] = odd
    base = QK_A
    v = y[:, base:base + LANES]
    va_ref[0] = _with_ones_column(v).astype(BF16)
    va_ref[1] = _with_ones_column(_swap_halves(v)).astype(BF16)
    base += LANES
    for c in range(B_HEADS // 2):
        a = y[:, base + c * LANES: base + (c + 1) * LANES] * qb_scale
        qb_ref[2 * c] = _low_half(a).astype(BF16)
        qb_ref[2 * c + 1] = _low_half(_swap_halves(a)).astype(BF16)
    base += B_HEADS * HEAD_DIM
    kk = y[:, base:base + LANES]
    kb_ref[0] = _low_half(kk).astype(BF16)
    kb_ref[1] = _low_half(_swap_halves(kk)).astype(BF16)
    base += LANES
    v = y[:, base:base + LANES]
    vb_ref[0] = _with_ones_column(v).astype(BF16)
    vb_ref[1] = _with_ones_column(_swap_halves(v)).astype(BF16)


def _rot_half_perm(n_cols, half):
    col = np.arange(n_cols)
    first = (col % (2 * half)) < half
    perm = np.where(first, col + half, col - half)
    sign = np.where(first, -1.0, 1.0).astype(np.float32)
    return perm, sign


def _inproj_even(x, g_mix, w_in, q_gain, k_gain, cos128, sin128, *, tm):
    b, s, d = x.shape
    q_scale = HEAD_DIM ** -0.5 * LOG2E
    perm, sign = _rot_half_perm(QK_A, HEAD_DIM // 4)
    w_all = w_in.astype(BF16)
    gains = jnp.concatenate([jnp.tile(q_gain * q_scale, A_HEADS), jnp.tile(k_gain, A_KV_HEADS)])
    gq = gains.reshape(1, QK_A)
    gr = (gains[perm] * sign).reshape(1, QK_A)
    blk = np.arange(QK_A) // HEAD_DIM
    avg = jnp.asarray((blk[:, None] == blk[None, :]).astype(np.float32) / HEAD_DIM, BF16)
    n_all = w_all.shape[1]

    def heads(n):
        return pl.BlockSpec((None, n, tm, LANES), lambda bi, i: (bi, 0, i, 0))

    def out(n):
        return jax.ShapeDtypeStruct((b, n, s, LANES), BF16)

    return pl.pallas_call(
        functools.partial(_inproj_even_kernel, qb_scale=q_scale),
        grid=(b, s // tm),
        in_specs=[
            pl.BlockSpec((None, tm, d), lambda bi, i: (bi, i, 0)),
            _const_spec((1, d)),
            _const_spec((d, n_all)),
            _const_spec((QK_A, QK_A)),
            _const_spec((1, QK_A)),
            _const_spec((1, QK_A)),
            pl.BlockSpec((tm, LANES), lambda bi, i: (i, 0)),
            pl.BlockSpec((tm, LANES), lambda bi, i: (i, 0)),
        ],
        out_specs=[heads(A_HEADS), heads(A_KV_HEADS), heads(A_KV_HEADS),
                   heads(B_HEADS), heads(B_KV_HEADS), heads(B_KV_HEADS)],
        out_shape=[out(A_HEADS), out(A_KV_HEADS), out(A_KV_HEADS),
                   out(B_HEADS), out(B_KV_HEADS), out(B_KV_HEADS)],
        compiler_params=_params("parallel", "parallel"),
        name="inproj_even",
    )(x, g_mix.reshape(1, d), w_all, avg, gq, gr, cos128, sin128)


def _inproj_latent_kernel(x_ref, g_ref, wd_ref, gq_ref, gkv_ref, wq_ref, wqr_ref, wkv_ref,
                          cos_ref, sin_ref, q_ref, k_ref, v_ref, *, q_scale, n_split):
    tm = x_ref.shape[0]
    per_tile = LANES // ROPE_DIM
    base = Q_LORA + KV_LORA
    for part in range(n_split):
        rs = slice(part * (tm // n_split), (part + 1) * (tm // n_split))
        h = _rms(x_ref[rs, :], g_ref[...]).astype(BF16)
        c = jnp.dot(h, wd_ref[...], preferred_element_type=F32)
        cq = _rms(c[:, :Q_LORA], gq_ref[...]).astype(BF16)
        ckv = _rms(c[:, Q_LORA:Q_LORA + KV_LORA], gkv_ref[...]).astype(BF16)
        cos = cos_ref[rs, :]
        sin = sin_ref[rs, :]
        k_rope = c[:, base:base + LANES] * cos + c[:, base + LANES:base + 2 * LANES] * sin
        yq = jnp.dot(cq, wq_ref[...], preferred_element_type=F32)
        yqr = jnp.dot(cq, wqr_ref[...], preferred_element_type=F32)
        ykv = jnp.dot(ckv, wkv_ref[...], preferred_element_type=F32)
        for hd in range(C_HEADS):
            sl = slice(hd * LANES, (hd + 1) * LANES)
            tile = yqr[:, (hd // per_tile) * LANES:(hd // per_tile + 1) * LANES]
            shift = (NOPE_DIM - (hd % per_tile) * ROPE_DIM) % LANES
            rot = pltpu.roll(tile, shift, axis=1) if shift else tile
            q_ref[hd, rs, :] = ((yq[:, sl] * cos + rot * sin) * q_scale).astype(BF16)
            kv = ykv[:, sl]
            k_ref[hd, rs, :] = jnp.where(_lane_ids(kv.shape) < NOPE_DIM, kv, k_rope).astype(BF16)
            v_ref[hd, rs, :] = _with_ones_column(_swap_halves(kv)).astype(BF16)


def _pad_heads(w, n_heads, width, lo, hi, at):
    k = w.shape[0]
    w3 = w.reshape(k, n_heads, width)[:, :, lo:hi]
    w3 = jnp.pad(w3, ((0, 0), (0, 0), (at, LANES - at - (hi - lo))))
    return w3.reshape(k, n_heads * LANES)


def _inproj_latent(x, g_mix, w_down, q_gain, kv_gain, w_uq, w_ukv, cos128, sin128, *, tm):
    b, s, d = x.shape
    qk_dim = NOPE_DIM + ROPE_DIM
    q_scale = qk_dim ** -0.5 * LOG2E
    perm, sign = _rot_half_perm(ROPE_DIM, ROPE_DIM // 2)
    base = Q_LORA + KV_LORA
    w_kr = w_down[:, base:]
    pad = ((0, 0), (NOPE_DIM, LANES - NOPE_DIM - ROPE_DIM))
    wd = jnp.concatenate([w_down[:, :base], jnp.pad(w_kr, pad), jnp.pad(w_kr[:, perm] * sign, pad)],
                         axis=1).astype(BF16)
    wq = _pad_heads(w_uq, C_HEADS, qk_dim, 0, qk_dim, 0).astype(BF16)
    w_qr = w_uq.reshape(Q_LORA, C_HEADS, qk_dim)[:, :, NOPE_DIM:][:, :, perm] * sign
    wqr = w_qr.reshape(Q_LORA, C_HEADS * ROPE_DIM).astype(BF16)
    wkv = w_ukv.astype(BF16)
    nq = C_HEADS * LANES
    head_spec = pl.BlockSpec((None, C_HEADS, tm, LANES), lambda bi, i: (bi, 0, i, 0))
    out = jax.ShapeDtypeStruct((b, C_HEADS, s, LANES), BF16)
    return pl.pallas_call(
        functools.partial(_inproj_latent_kernel, q_scale=q_scale, n_split=2),
        grid=(b, s // tm),
        in_specs=[
            pl.BlockSpec((None, tm, d), lambda bi, i: (bi, i, 0)),
            _const_spec((1, d)),
            _const_spec(wd.shape),
            _const_spec((1, Q_LORA)),
            _const_spec((1, KV_LORA)),
            _const_spec((Q_LORA, nq)),
            _const_spec(wqr.shape),
            _const_spec(wkv.shape),
            pl.BlockSpec((tm, LANES), lambda bi, i: (i, 0)),
            pl.BlockSpec((tm, LANES), lambda bi, i: (i, 0)),
        ],
        out_specs=[head_spec, head_spec, head_spec],
        out_shape=[out, out, out],
        compiler_params=_params("parallel", "parallel"),
        name="inproj_latent",
    )(x, g_mix.reshape(1, d), wd, q_gain.reshape(1, Q_LORA), kv_gain.reshape(1, KV_LORA),
      wq, wqr, wkv, cos128, sin128)


SCORE_RING = 4
SCORE_AHEAD = 2
Q_BLOCKS_PER_STEP = 4


def _flash_rows(q, k_ref, v_ref, kv, s_ref, acc_ref, *, tk):
    rows = q.shape[0]
    n_chunks = k_ref.shape[1] // tk
    ring = s_ref.shape[0]
    assert SCORE_AHEAD < ring <= n_chunks
    n_trips = (n_chunks - SCORE_AHEAD) // ring
    acc_ref[...] = jnp.zeros_like(acc_ref)

    def scores(j, slot):
        start = pl.multiple_of(j * tk, tk)
        kc = k_ref[kv, pl.ds(start, tk), :]
        s_ref[slot] = lax.dot_general(q, kc, (((1,), (1,)), ((), ())), preferred_element_type=F32)

    def softmax_pv(j, slot, m):
        start = pl.multiple_of(j * tk, tk)
        vc = v_ref[kv, pl.ds(start, tk), :]
        m_new = jnp.maximum(m, jnp.max(s_ref[slot], axis=1, keepdims=True))
        p = jnp.exp2(s_ref[slot] - m_new).astype(BF16)
        alpha = jnp.exp2(m - m_new)
        acc_ref[...] = acc_ref[...] * alpha + jnp.dot(p, vc, preferred_element_type=F32)
        return m_new

    def body(i, m):
        for r in range(ring):
            scores(i * ring + r + SCORE_AHEAD, (r + SCORE_AHEAD) % ring)
            m = softmax_pv(i * ring + r, r, m)
        return m

    for j in range(SCORE_AHEAD):
        scores(j, j)
    m = lax.fori_loop(0, n_trips, body, jnp.full((rows, 1), -jnp.inf, F32))
    for j in range(n_trips * ring, n_chunks):
        if j + SCORE_AHEAD < n_chunks:
            scores(j + SCORE_AHEAD, (j + SCORE_AHEAD) % ring)
        m = softmax_pv(j, j % ring, m)
    return acc_ref[...]


def _normalise(acc):
    return acc / acc[:, V_DIM:V_DIM + 1]


def _flash_kernel(q_ref, k_ref, v_ref, o_ref, s_ref, acc_ref, *, tq, tk, shared_kv):
    def q_block(bi, carry):
        r0 = pl.multiple_of(bi * tq, tq)
        if shared_kv:
            q = q_ref[:, pl.ds(r0, tq), :].reshape(2 * tq, LANES)
            o = _normalise(_flash_rows(q, k_ref, v_ref, 0, s_ref, acc_ref, tk=tk))
            o0, o1 = o[:tq], o[tq:]
        else:
            o0 = _normalise(_flash_rows(q_ref[0, pl.ds(r0, tq), :], k_ref, v_ref, 0, s_ref, acc_ref, tk=tk))
            o1 = _normalise(_flash_rows(q_ref[1, pl.ds(r0, tq), :], k_ref, v_ref, 1, s_ref, acc_ref, tk=tk))
        lane = _lane_ids(o0.shape)
        o_ref[pl.ds(r0, tq), :] = jnp.where(lane < V_DIM, o0, _swap_halves(o1)).astype(o_ref.dtype)
        return carry

    lax.fori_loop(0, q_ref.shape[1] // tq, q_block, 0)


def _flash_attention(q, k, v, *, tq, tk):
    b, h, s, _ = q.shape
    hkv = k.shape[1]
    shared_kv = h // hkv >= 2
    if shared_kv:
        group_pairs = h // hkv // 2
        kv_spec = pl.BlockSpec((None, 1, s, LANES), lambda bi, p, i: (bi, p // group_pairs, 0, 0))
        rows = 2 * tq
    else:
        kv_spec = pl.BlockSpec((None, 2, s, LANES), lambda bi, p, i: (bi, p, 0, 0),
                               pipeline_mode=pl.Buffered(1))
        rows = tq
    step_q = min(Q_BLOCKS_PER_STEP * tq, s)
    return pl.pallas_call(
        functools.partial(_flash_kernel, tq=tq, tk=tk, shared_kv=shared_kv),
        grid=(b, h // 2, s // step_q),
        in_specs=[
            pl.BlockSpec((None, 2, step_q, LANES), lambda bi, p, i: (bi, p, i, 0)),
            kv_spec,
            kv_spec,
        ],
        out_specs=pl.BlockSpec((None, step_q, LANES), lambda bi, p, i: (bi, i, p)),
        out_shape=jax.ShapeDtypeStruct((b, s, h * V_DIM), BF16),
        scratch_shapes=[pltpu.VMEM((SCORE_RING, rows, tk), F32), pltpu.VMEM((rows, LANES), F32)],
        compiler_params=_params("parallel", "parallel", "arbitrary"),
        name="flash_shared" if shared_kv else "flash_mha",
    )(q, k, v)


def _window_kernel(sink_ref, q_ref, kp_ref, km_ref, kn_ref, vp_ref, vm_ref, vn_ref, o_ref,
                   *, tq, seq, group):
    i = pl.program_id(1)
    nk = tq + 2 * WINDOW
    row = lax.broadcasted_iota(jnp.int32, (tq, nk), 0)
    col = lax.broadcasted_iota(jnp.int32, (tq, nk), 1)
    dist = jnp.abs(col - WINDOW - row)
    kpos = i * tq - WINDOW + col
    valid = (dist <= WINDOW) & (kpos >= 0) & (kpos < seq)
    dist_f = dist.astype(F32)
    lane = _lane_ids((tq, LANES))
    pairs = []
    for g in range(q_ref.shape[0] // group):
        kcat = jnp.concatenate([kp_ref[g], km_ref[g], kn_ref[g]], axis=0)
        vcat = jnp.concatenate([vp_ref[g], vm_ref[g], vn_ref[g]], axis=0)
        q = q_ref[g * group:(g + 1) * group].reshape(group * tq, LANES)
        s_all = lax.dot_general(q, kcat, (((1,), (1,)), ((), ())), preferred_element_type=F32)
        outs = []
        for hh in range(group):
            head = g * group + hh
            slope = 2.0 ** (-8.0 * (head + 1) / B_HEADS) * LOG2E
            sink = sink_ref[head] * LOG2E
            sc = jnp.where(valid, s_all[hh * tq:(hh + 1) * tq] - slope * dist_f, -jnp.inf)
            m = jnp.maximum(jnp.max(sc, axis=1, keepdims=True), sink)
            e = jnp.exp2(sc - m).astype(BF16)
            acc = jnp.dot(e, vcat, preferred_element_type=F32)
            denom = acc[:, V_DIM:V_DIM + 1] + jnp.exp2(sink - m)
            outs.append(acc / denom)
        pairs += [jnp.where(lane < V_DIM, outs[2 * c], _swap_halves(outs[2 * c + 1]))
                  for c in range(group // 2)]
    o_ref[...] = jnp.concatenate(pairs, axis=1).astype(o_ref.dtype)


def _window_attention(q, k, v, sink, *, tq):
    b, h, s, _ = q.shape
    hkv = k.shape[1]
    per = tq // WINDOW
    last = s // WINDOW - 1

    def main(bi, i):
        return (bi, 0, i, 0)

    def prev(bi, i):
        return (bi, 0, jnp.maximum(i * per - 1, 0), 0)

    def nxt(bi, i):
        return (bi, 0, jnp.minimum((i + 1) * per, last), 0)

    halo = lambda f: pl.BlockSpec((None, hkv, WINDOW, LANES), f)
    mid = pl.BlockSpec((None, hkv, tq, LANES), main)
    return pl.pallas_call(
        functools.partial(_window_kernel, tq=tq, seq=s, group=h // hkv),
        grid=(b, s // tq),
        in_specs=[
            pl.BlockSpec(memory_space=pltpu.SMEM),
            pl.BlockSpec((None, h, tq, LANES), main),
            halo(prev), mid, halo(nxt),
            halo(prev), mid, halo(nxt),
        ],
        out_specs=pl.BlockSpec((None, tq, h * V_DIM), lambda bi, i: (bi, i, 0)),
        out_shape=jax.ShapeDtypeStruct((b, s, h * V_DIM), BF16),
        compiler_params=_params("parallel", "parallel"),
        name="window_attention",
    )(sink, q, k, k, k, v, v, v)


HALO = 16


def _mix_ffn_kernel(*refs, tm, n_acts, final_norm):
    x_refs = refs[0:3]
    act_refs = [refs[3 + 3 * a: 6 + 3 * a] for a in range(n_acts)]
    base = 3 + 3 * n_acts
    w_refs = refs[base:base + n_acts]
    g_ref, wg_ref, wv_ref, cw_ref, cb_ref, wd_ref, gf_ref, o_ref = refs[base + n_acts:]
    i = pl.program_id(1)
    ext = tm + 2 * HALO
    x1 = jnp.concatenate([r[...] for r in x_refs], axis=0)
    for triple, w_ref in zip(act_refs, w_refs):
        a = jnp.concatenate([r[...] for r in triple], axis=0)
        x1 = x1 + jnp.dot(a, w_ref[...], preferred_element_type=F32)
    row = lax.broadcasted_iota(jnp.int32, (ext, 1), 0)
    inside = ((row >= HALO) | (i > 0)) & ((row < HALO + tm) | (i < pl.num_programs(1) - 1))
    x1 = jnp.where(inside, x1, 0.0)
    h = _rms(x1, g_ref[...]).astype(BF16)
    mid = slice(HALO, HALO + tm)
    gate = jnp.dot(h, wg_ref[...], preferred_element_type=F32)
    val = jnp.dot(h[mid], wv_ref[...], preferred_element_type=F32)
    g_prev = pltpu.roll(gate, 1, axis=0)[mid]
    g_next = pltpu.roll(gate, ext - 1, axis=0)[mid]
    gc = cb_ref[...] + g_prev * cw_ref[0:1, :] + gate[mid] * cw_ref[1:2, :] + g_next * cw_ref[2:3, :]
    act = (gc * jax.nn.sigmoid(gc) * val).astype(BF16)
    out = x1[mid] + jnp.dot(act, wd_ref[...], preferred_element_type=F32)
    if final_norm:
        out = _rms(out, gf_ref[...])
    o_ref[...] = out


def _mix_ffn(x, acts, w_outs, g_ffn, w_up, conv_w, conv_b, w_down, g_final, *, tm, final_norm):
    b, s, d = x.shape
    per = tm // HALO
    last = s // HALO - 1
    wg = w_up[:, :D_FF].astype(BF16)
    wv = w_up[:, D_FF:].astype(BF16)
    wd = w_down.astype(BF16)

    def triple(width):
        return [
            pl.BlockSpec((None, HALO, width), lambda bi, i: (bi, jnp.maximum(i * per - 1, 0), 0)),
            pl.BlockSpec((None, tm, width), lambda bi, i: (bi, i, 0)),
            pl.BlockSpec((None, HALO, width), lambda bi, i: (bi, jnp.minimum((i + 1) * per, last), 0)),
        ]

    in_specs = triple(d)
    operands = [x, x, x]
    for a in acts:
        in_specs += triple(a.shape[-1])
        operands += [a, a, a]
    consts = list(w_outs) + [g_ffn.reshape(1, d), wg, wv, conv_w, conv_b.reshape(1, D_FF), wd,
                             g_final.reshape(1, d)]
    in_specs += [_const_spec(c.shape) for c in consts]
    return pl.pallas_call(
        functools.partial(_mix_ffn_kernel, tm=tm, n_acts=len(acts), final_norm=final_norm),
        grid=(b, s // tm),
        in_specs=in_specs,
        out_specs=pl.BlockSpec((None, tm, d), lambda bi, i: (bi, i, 0)),
        out_shape=jax.ShapeDtypeStruct((b, s, d), F32),
        compiler_params=_params("parallel", "parallel"),
        name="mix_ffn_final" if final_norm else "mix_ffn",
    )(*operands, *consts)


def _rope_angles(pos, dim):
    inv_freq = ROPE_THETA ** (-jnp.arange(0, dim, 2, dtype=F32) / dim)
    ang = pos.astype(F32)[:, None] * inv_freq[None, :]
    return jnp.cos(ang), jnp.sin(ang)


def _rope_tables(s):
    t = jnp.arange(s)
    n_rows = s // GRID_W
    cos_r, sin_r = (jnp.repeat(a, GRID_W, axis=0) for a in _rope_angles(jnp.arange(n_rows), HEAD_DIM // 2))
    cos_c, sin_c = (jnp.tile(a, (n_rows, 1)) for a in _rope_angles(jnp.arange(GRID_W), HEAD_DIM // 2))
    cos_ax = jnp.tile(jnp.concatenate([cos_r, cos_r, cos_c, cos_c], axis=1), (1, 2))
    sin_ax = jnp.tile(jnp.concatenate([sin_r, sin_r, sin_c, sin_c], axis=1), (1, 2))
    cos_t, sin_t = _rope_angles(t, ROPE_DIM)
    tail = jnp.zeros((s, LANES - NOPE_DIM - ROPE_DIM), F32)
    cos_l = jnp.concatenate([jnp.ones((s, NOPE_DIM), F32), cos_t, cos_t, tail], axis=1)
    sin_l = jnp.concatenate([jnp.zeros((s, NOPE_DIM), F32), sin_t, sin_t, tail], axis=1)
    return cos_ax, sin_ax, cos_l, sin_l


TM = 512
TQ_SHARED = 512
TQ_MHA = 1024
TK = 1024
TQ_WINDOW = 256


def _trunk(x, p, tables):
    cos_ax, sin_ax, cos_l, sin_l = tables
    s = x.shape[1]
    tm = min(TM, s)
    qa, ka, va, qb, kb, vb = _inproj_even(x, p["norm_mix"][0], p["e_w_in"][0], p["e_q_gain"][0],
                                          p["e_k_gain"][0], cos_ax, sin_ax, tm=tm)
    oa = _flash_attention(qa, ka, va, tq=min(TQ_SHARED, s), tk=min(TK, s // 8))
    ob = _window_attention(qb, kb, vb, p["e_sink"][0], tq=min(TQ_WINDOW, s))
    w_out = p["e_w_out"][0].astype(BF16)
    n_a = A_HEADS * HEAD_DIM
    x = _mix_ffn(x, [oa, ob], [w_out[:n_a], w_out[n_a:]], p["norm_ffn"][0], p["f_w_up"][0], p["f_conv_w"][0],
                 p["f_conv_b"][0], p["f_w_down"][0], p["norm_final"], tm=tm, final_norm=False)
    q, k, v = _inproj_latent(x, p["norm_mix"][1], p["o_w_down"][0], p["o_q_gain"][0], p["o_kv_gain"][0],
                             p["o_w_uq"][0], p["o_w_ukv"][0], cos_l, sin_l, tm=tm)
    o = _flash_attention(q, k, v, tq=min(TQ_MHA, s), tk=min(TK, s // 8))
    x = _mix_ffn(x, [o], [p["o_w_out"][0].astype(BF16)], p["norm_ffn"][1], p["f_w_up"][1], p["f_conv_w"][1],
                 p["f_conv_b"][1], p["f_w_down"][1], p["norm_final"], tm=tm, final_norm=True)
    return x


def kernel(x_prompt, x_sample, norm_mix, norm_ffn, norm_final, e_w_in, e_q_gain, e_k_gain, e_sink, e_w_out, o_w_down, o_q_gain, o_kv_gain, o_w_uq, o_w_ukv, o_w_out, f_w_up, f_conv_w, f_conv_b, f_w_down):
    p = dict(norm_mix=norm_mix, norm_ffn=norm_ffn, norm_final=norm_final, e_w_in=e_w_in, e_q_gain=e_q_gain,
             e_k_gain=e_k_gain, e_sink=e_sink, e_w_out=e_w_out, o_w_down=o_w_down, o_q_gain=o_q_gain,
             o_kv_gain=o_kv_gain, o_w_uq=o_w_uq, o_w_ukv=o_w_ukv, o_w_out=o_w_out, f_w_up=f_w_up,
             f_conv_w=f_conv_w, f_conv_b=f_conv_b, f_w_down=f_w_down)
    tables = _rope_tables(x_prompt.shape[1])
    return _trunk(x_prompt, p, tables), _trunk(x_sample, p, tables)
```

```python
import functools
import math

import numpy as np
import jax
import jax.numpy as jnp
from jax import lax
from jax.experimental import pallas as pl
from jax.experimental.pallas import tpu as pltpu

D_MODEL = 1024
GRID_W = 64
HEAD_DIM = 64
A_HEADS = 8
A_KV_HEADS = 2
B_HEADS = 8
B_KV_HEADS = 2
WINDOW = 128
C_HEADS = 16
Q_LORA = 384
KV_LORA = 256
NOPE_DIM = 64
ROPE_DIM = 32
V_DIM = 64
D_FF = 2816
CONV_W = 3
ROPE_THETA = 10000.0
EPS = 1e-6

LANES = 128
LOG2E = math.log2(math.e)
VMEM_LIMIT = 56 * 1024 * 1024

F32 = jnp.float32
BF16 = jnp.bfloat16

QK_A = (A_HEADS + A_KV_HEADS) * HEAD_DIM


def _params(*sem):
    return pltpu.CompilerParams(dimension_semantics=sem, vmem_limit_bytes=VMEM_LIMIT)


def _const_spec(shape):
    nd = len(shape)
    return pl.BlockSpec(shape, lambda *_: (0,) * nd, pipeline_mode=pl.Buffered(1))


def _rms(x, gain):
    ms = jnp.mean(x * x, axis=-1, keepdims=True)
    return x * lax.rsqrt(ms + EPS) * gain


def _lane_ids(shape):
    return lax.broadcasted_iota(jnp.int32, shape, len(shape) - 1)


def _low_half(x):
    return jnp.where(_lane_ids(x.shape) < HEAD_DIM, x, 0.0)


def _swap_halves(x):
    return pltpu.roll(x, HEAD_DIM, axis=1)


def _with_ones_column(v):
    lane = _lane_ids(v.shape)
    return jnp.where(lane < V_DIM, v, jnp.where(lane == V_DIM, 1.0, 0.0))


def _inproj_even_kernel(x_ref, g_ref, w_ref, avg_ref, gq_ref, gr_ref, cos_ref, sin_ref,
                        qa_ref, ka_ref, va_ref, qb_ref, kb_ref, vb_ref, *, qb_scale):
    h = _rms(x_ref[...], g_ref[...]).astype(BF16)
    y = jnp.dot(h, w_ref[...], preferred_element_type=F32)
    yqk = y[:, :QK_A]
    sq = yqk * yqk
    hi = sq.astype(BF16)
    lo = (sq - hi.astype(F32)).astype(BF16)
    msq = (jnp.dot(hi, avg_ref[...], preferred_element_type=F32)
           + jnp.dot(lo, avg_ref[...], preferred_element_type=F32))
    r = lax.rsqrt(msq + EPS)
    cos = cos_ref[...]
    sin = sin_ref[...]
    rot_half = HEAD_DIM // 4
    first = _lane_ids(cos.shape) % (2 * rot_half) < rot_half
    for c in range(QK_A // LANES):
        sl = slice(c * LANES, (c + 1) * LANES)
        yc = yqk[:, sl]
        swapped = jnp.where(first, pltpu.roll(yc, LANES - rot_half, axis=1), pltpu.roll(yc, rot_half, axis=1))
        a = (yc * (gq_ref[:, sl] * cos) + swapped * (gr_ref[:, sl] * sin)) * r[:, sl]
        even = _low_half(a).astype(BF16)
        odd = _low_half(_swap_halves(a)).astype(BF16)
        if c < A_HEADS // 2:
            qa_ref[2 * c] = even
            qa_ref[2 * c + 1] = odd
        else:
            ka_ref[0] = even
            ka_ref[1] = odd
    base = QK_A
    v = y[:, base:base + LANES]
    va_ref[0] = _with_ones_column(v).astype(BF16)
    va_ref[1] = _with_ones_column(_swap_halves(v)).astype(BF16)
    base += LANES
    for c in range(B_HEADS // 2):
        a = y[:, base + c * LANES: base + (c + 1) * LANES] * qb_scale
        qb_ref[2 * c] = _low_half(a).astype(BF16)
        qb_ref[2 * c + 1] = _low_half(_swap_halves(a)).astype(BF16)
    base += B_HEADS * HEAD_DIM
    kk = y[:, base:base + LANES]
    kb_ref[0] = _low_half(kk).astype(BF16)
    kb_ref[1] = _low_half(_swap_halves(kk)).astype(BF16)
    base += LANES
    v = y[:, base:base + LANES]
    vb_ref[0] = _with_ones_column(v).astype(BF16)
    vb_ref[1] = _with_ones_column(_swap_halves(v)).astype(BF16)


def _rot_half_perm(n_cols, half):
    col = np.arange(n_cols)
    first = (col % (2 * half)) < half
    perm = np.where(first, col + half, col - half)
    sign = np.where(first, -1.0, 1.0).astype(np.float32)
    return perm, sign


def _inproj_even(x, g_mix, w_in, q_gain, k_gain, cos128, sin128, *, tm):
    b, s, d = x.shape
    q_scale = HEAD_DIM ** -0.5 * LOG2E
    perm, sign = _rot_half_perm(QK_A, HEAD_DIM // 4)
    w_all = w_in.astype(BF16)
    gains = jnp.concatenate([jnp.tile(q_gain * q_scale, A_HEADS), jnp.tile(k_gain, A_KV_HEADS)])
    gq = gains.reshape(1, QK_A)
    gr = (gains[perm] * sign).reshape(1, QK_A)
    blk = np.arange(QK_A) // HEAD_DIM
    avg = jnp.asarray((blk[:, None] == blk[None, :]).astype(np.float32) / HEAD_DIM, BF16)
    n_all = w_all.shape[1]

    def heads(n):
        return pl.BlockSpec((None, n, tm, LANES), lambda bi, i: (bi, 0, i, 0))

    def out(n):
        return jax.ShapeDtypeStruct((b, n, s, LANES), BF16)

    return pl.pallas_call(
        functools.partial(_inproj_even_kernel, qb_scale=q_scale),
        grid=(b, s // tm),
        in_specs=[
            pl.BlockSpec((None, tm, d), lambda bi, i: (bi, i, 0)),
            _const_spec((1, d)),
            _const_spec((d, n_all)),
            _const_spec((QK_A, QK_A)),
            _const_spec((1, QK_A)),
            _const_spec((1, QK_A)),
            pl.BlockSpec((tm, LANES), lambda bi, i: (i, 0)),
            pl.BlockSpec((tm, LANES), lambda bi, i: (i, 0)),
        ],
        out_specs=[heads(A_HEADS), heads(A_KV_HEADS), heads(A_KV_HEADS),
                   heads(B_HEADS), heads(B_KV_HEADS), heads(B_KV_HEADS)],
        out_shape=[out(A_HEADS), out(A_KV_HEADS), out(A_KV_HEADS),
                   out(B_HEADS), out(B_KV_HEADS), out(B_KV_HEADS)],
        compiler_params=_params("parallel", "parallel"),
        name="inproj_even",
    )(x, g_mix.reshape(1, d), w_all, avg, gq, gr, cos128, sin128)


def _inproj_latent_kernel(x_ref, g_ref, wd_ref, gq_ref, gkv_ref, wq_ref, wqr_ref, wkv_ref,
                          cos_ref, sin_ref, q_ref, k_ref, v_ref, *, q_scale, n_split):
    tm = x_ref.shape[0]
    per_tile = LANES // ROPE_DIM
    base = Q_LORA + KV_LORA
    for part in range(n_split):
        rs = slice(part * (tm // n_split), (part + 1) * (tm // n_split))
        h = _rms(x_ref[rs, :], g_ref[...]).astype(BF16)
        c = jnp.dot(h, wd_ref[...], preferred_element_type=F32)
        cq = _rms(c[:, :Q_LORA], gq_ref[...]).astype(BF16)
        ckv = _rms(c[:, Q_LORA:Q_LORA + KV_LORA], gkv_ref[...]).astype(BF16)
        cos = cos_ref[rs, :]
        sin = sin_ref[rs, :]
        k_rope = c[:, base:base + LANES] * cos + c[:, base + LANES:base + 2 * LANES] * sin
        yq = jnp.dot(cq, wq_ref[...], preferred_element_type=F32)
        yqr = jnp.dot(cq, wqr_ref[...], preferred_element_type=F32)
        ykv = jnp.dot(ckv, wkv_ref[...], preferred_element_type=F32)
        for hd in range(C_HEADS):
            sl = slice(hd * LANES, (hd + 1) * LANES)
            tile = yqr[:, (hd // per_tile) * LANES:(hd // per_tile + 1) * LANES]
            shift = (NOPE_DIM - (hd % per_tile) * ROPE_DIM) % LANES
            rot = pltpu.roll(tile, shift, axis=1) if shift else tile
            q_ref[hd, rs, :] = ((yq[:, sl] * cos + rot * sin) * q_scale).astype(BF16)
            kv = ykv[:, sl]
            k_ref[hd, rs, :] = jnp.where(_lane_ids(kv.shape) < NOPE_DIM, kv, k_rope).astype(BF16)
            v_ref[hd, rs, :] = _with_ones_column(_swap_halves(kv)).astype(BF16)


def _pad_heads(w, n_heads, width, lo, hi, at):
    k = w.shape[0]
    w3 = w.reshape(k, n_heads, width)[:, :, lo:hi]
    w3 = jnp.pad(w3, ((0, 0), (0, 0), (at, LANES - at - (hi - lo))))
    return w3.reshape(k, n_heads * LANES)


def _inproj_latent(x, g_mix, w_down, q_gain, kv_gain, w_uq, w_ukv, cos128, sin128, *, tm):
    b, s, d = x.shape
    qk_dim = NOPE_DIM + ROPE_DIM
    q_scale = qk_dim ** -0.5 * LOG2E
    perm, sign = _rot_half_perm(ROPE_DIM, ROPE_DIM // 2)
    base = Q_LORA + KV_LORA
    w_kr = w_down[:, base:]
    pad = ((0, 0), (NOPE_DIM, LANES - NOPE_DIM - ROPE_DIM))
    wd = jnp.concatenate([w_down[:, :base], jnp.pad(w_kr, pad), jnp.pad(w_kr[:, perm] * sign, pad)],
                         axis=1).astype(BF16)
    wq = _pad_heads(w_uq, C_HEADS, qk_dim, 0, qk_dim, 0).astype(BF16)
    w_qr = w_uq.reshape(Q_LORA, C_HEADS, qk_dim)[:, :, NOPE_DIM:][:, :, perm] * sign
    wqr = w_qr.reshape(Q_LORA, C_HEADS * ROPE_DIM).astype(BF16)
    wkv = w_ukv.astype(BF16)
    nq = C_HEADS * LANES
    head_spec = pl.BlockSpec((None, C_HEADS, tm, LANES), lambda bi, i: (bi, 0, i, 0))
    out = jax.ShapeDtypeStruct((b, C_HEADS, s, LANES), BF16)
    return pl.pallas_call(
        functools.partial(_inproj_latent_kernel, q_scale=q_scale, n_split=2),
        grid=(b, s // tm),
        in_specs=[
            pl.BlockSpec((None, tm, d), lambda bi, i: (bi, i, 0)),
            _const_spec((1, d)),
            _const_spec(wd.shape),
            _const_spec((1, Q_LORA)),
            _const_spec((1, KV_LORA)),
            _const_spec((Q_LORA, nq)),
            _const_spec(wqr.shape),
            _const_spec(wkv.shape),
            pl.BlockSpec((tm, LANES), lambda bi, i: (i, 0)),
            pl.BlockSpec((tm, LANES), lambda bi, i: (i, 0)),
        ],
        out_specs=[head_spec, head_spec, head_spec],
        out_shape=[out, out, out],
        compiler_params=_params("parallel", "parallel"),
        name="inproj_latent",
    )(x, g_mix.reshape(1, d), wd, q_gain.reshape(1, Q_LORA), kv_gain.reshape(1, KV_LORA),
      wq, wqr, wkv, cos128, sin128)


SCORE_RING = 4
SCORE_AHEAD = 2


def _flash_rows(q, k_ref, v_ref, kv, s_ref, acc_ref, *, tk):
    rows = q.shape[0]
    n_chunks = k_ref.shape[1] // tk
    ring = s_ref.shape[0]
    assert SCORE_AHEAD < ring <= n_chunks
    n_trips = (n_chunks - SCORE_AHEAD) // ring
    acc_ref[...] = jnp.zeros_like(acc_ref)

    def scores(j, slot):
        start = pl.multiple_of(j * tk, tk)
        kc = k_ref[kv, pl.ds(start, tk), :]
        s_ref[slot] = lax.dot_general(q, kc, (((1,), (1,)), ((), ())), preferred_element_type=F32)

    def softmax_pv(j, slot, m):
        start = pl.multiple_of(j * tk, tk)
        vc = v_ref[kv, pl.ds(start, tk), :]
        m_new = jnp.maximum(m, jnp.max(s_ref[slot], axis=1, keepdims=True))
        p = jnp.exp2(s_ref[slot] - m_new).astype(BF16)
        alpha = jnp.exp2(m - m_new)
        acc_ref[...] = acc_ref[...] * alpha + jnp.dot(p, vc, preferred_element_type=F32)
        return m_new

    def body(i, m):
        for r in range(ring):
            scores(i * ring + r + SCORE_AHEAD, (r + SCORE_AHEAD) % ring)
            m = softmax_pv(i * ring + r, r, m)
        return m

    for j in range(SCORE_AHEAD):
        scores(j, j)
    m = lax.fori_loop(0, n_trips, body, jnp.full((rows, 1), -jnp.inf, F32))
    for j in range(n_trips * ring, n_chunks):
        if j + SCORE_AHEAD < n_chunks:
            scores(j + SCORE_AHEAD, (j + SCORE_AHEAD) % ring)
        m = softmax_pv(j, j % ring, m)
    return acc_ref[...]


def _normalise(acc):
    return acc / acc[:, V_DIM:V_DIM + 1]


def _flash_kernel(q_ref, k_ref, v_ref, o_ref, s_ref, acc_ref, *, tq, tk, shared_kv):
    if shared_kv:
        q = q_ref[...].reshape(2 * tq, LANES)
        o = _normalise(_flash_rows(q, k_ref, v_ref, 0, s_ref, acc_ref, tk=tk))
        o0, o1 = o[:tq], o[tq:]
    else:
        o0 = _normalise(_flash_rows(q_ref[0], k_ref, v_ref, 0, s_ref, acc_ref, tk=tk))
        o1 = _normalise(_flash_rows(q_ref[1], k_ref, v_ref, 1, s_ref, acc_ref, tk=tk))
    lane = _lane_ids(o0.shape)
    o_ref[...] = jnp.where(lane < V_DIM, o0, _swap_halves(o1)).astype(o_ref.dtype)


def _flash_attention(q, k, v, *, tq, tk):
    b, h, s, _ = q.shape
    hkv = k.shape[1]
    shared_kv = h // hkv >= 2
    if shared_kv:
        group_pairs = h // hkv // 2
        kv_spec = pl.BlockSpec((None, 1, s, LANES), lambda bi, p, i: (bi, p // group_pairs, 0, 0))
        k_spec = kv_spec
        rows = 2 * tq
    else:
        k_spec = pl.BlockSpec((None, 2, s, LANES), lambda bi, p, i: (bi, p, 0, 0))
        kv_spec = pl.BlockSpec((None, 2, s, LANES), lambda bi, p, i: (bi, p, 0, 0),
                               pipeline_mode=pl.Buffered(1))
        rows = tq
    return pl.pallas_call(
        functools.partial(_flash_kernel, tq=tq, tk=tk, shared_kv=shared_kv),
        grid=(b, h // 2, s // tq),
        in_specs=[
            pl.BlockSpec((None, 2, tq, LANES), lambda bi, p, i: (bi, p, i, 0)),
            k_spec,
            kv_spec,
        ],
        out_specs=pl.BlockSpec((None, tq, LANES), lambda bi, p, i: (bi, i, p)),
        out_shape=jax.ShapeDtypeStruct((b, s, h * V_DIM), BF16),
        scratch_shapes=[pltpu.VMEM((SCORE_RING, rows, tk), F32), pltpu.VMEM((rows, LANES), F32)],
        compiler_params=_params("parallel", "parallel", "arbitrary"),
        name="flash_shared" if shared_kv else "flash_mha",
    )(q, k, v)


def _window_kernel(sink_ref, q_ref, kp_ref, km_ref, kn_ref, vp_ref, vm_ref, vn_ref, o_ref,
                   *, tq, seq, group):
    i = pl.program_id(1)
    nk = tq + 2 * WINDOW
    row = lax.broadcasted_iota(jnp.int32, (tq, nk), 0)
    col = lax.broadcasted_iota(jnp.int32, (tq, nk), 1)
    dist = jnp.abs(col - WINDOW - row)
    kpos = i * tq - WINDOW + col
    valid = (dist <= WINDOW) & (kpos >= 0) & (kpos < seq)
    dist_f = dist.astype(F32)
    lane = _lane_ids((tq, LANES))
    pairs = []
    for g in range(q_ref.shape[0] // group):
        kcat = jnp.concatenate([kp_ref[g], km_ref[g], kn_ref[g]], axis=0)
        vcat = jnp.concatenate([vp_ref[g], vm_ref[g], vn_ref[g]], axis=0)
        q = q_ref[g * group:(g + 1) * group].reshape(group * tq, LANES)
        s_all = lax.dot_general(q, kcat, (((1,), (1,)), ((), ())), preferred_element_type=F32)
        outs = []
        for hh in range(group):
            head = g * group + hh
            slope = 2.0 ** (-8.0 * (head + 1) / B_HEADS) * LOG2E
            sink = sink_ref[head] * LOG2E
            sc = jnp.where(valid, s_all[hh * tq:(hh + 1) * tq] - slope * dist_f, -jnp.inf)
            m = jnp.maximum(jnp.max(sc, axis=1, keepdims=True), sink)
            e = jnp.exp2(sc - m).astype(BF16)
            acc = jnp.dot(e, vcat, preferred_element_type=F32)
            denom = acc[:, V_DIM:V_DIM + 1] + jnp.exp2(sink - m)
            outs.append(acc / denom)
        pairs += [jnp.where(lane < V_DIM, outs[2 * c], _swap_halves(outs[2 * c + 1]))
                  for c in range(group // 2)]
    o_ref[...] = jnp.concatenate(pairs, axis=1).astype(o_ref.dtype)


def _window_attention(q, k, v, sink, *, tq):
    b, h, s, _ = q.shape
    hkv = k.shape[1]
    per = tq // WINDOW
    last = s // WINDOW - 1

    def main(bi, i):
        return (bi, 0, i, 0)

    def prev(bi, i):
        return (bi, 0, jnp.maximum(i * per - 1, 0), 0)

    def nxt(bi, i):
        return (bi, 0, jnp.minimum((i + 1) * per, last), 0)

    halo = lambda f: pl.BlockSpec((None, hkv, WINDOW, LANES), f)
    mid = pl.BlockSpec((None, hkv, tq, LANES), main)
    return pl.pallas_call(
        functools.partial(_window_kernel, tq=tq, seq=s, group=h // hkv),
        grid=(b, s // tq),
        in_specs=[
            pl.BlockSpec(memory_space=pltpu.SMEM),
            pl.BlockSpec((None, h, tq, LANES), main),
            halo(prev), mid, halo(nxt),
            halo(prev), mid, halo(nxt),
        ],
        out_specs=pl.BlockSpec((None, tq, h * V_DIM), lambda bi, i: (bi, i, 0)),
        out_shape=jax.ShapeDtypeStruct((b, s, h * V_DIM), BF16),
        compiler_params=_params("parallel", "parallel"),
        name="window_attention",
    )(sink, q, k, k, k, v, v, v)


HALO = 16


def _mix_ffn_kernel(*refs, tm, n_acts, final_norm):
    x_refs = refs[0:3]
    act_refs = [refs[3 + 3 * a: 6 + 3 * a] for a in range(n_acts)]
    base = 3 + 3 * n_acts
    w_refs = refs[base:base + n_acts]
    g_ref, wg_ref, wv_ref, cw_ref, cb_ref, wd_ref, gf_ref, o_ref = refs[base + n_acts:]
    i = pl.program_id(1)
    ext = tm + 2 * HALO
    x1 = jnp.concatenate([r[...] for r in x_refs], axis=0)
    for triple, w_ref in zip(act_refs, w_refs):
        a = jnp.concatenate([r[...] for r in triple], axis=0)
        x1 = x1 + jnp.dot(a, w_ref[...], preferred_element_type=F32)
    row = lax.broadcasted_iota(jnp.int32, (ext, 1), 0)
    inside = ((row >= HALO) | (i > 0)) & ((row < HALO + tm) | (i < pl.num_programs(1) - 1))
    x1 = jnp.where(inside, x1, 0.0)
    h = _rms(x1, g_ref[...]).astype(BF16)
    mid = slice(HALO, HALO + tm)
    gate = jnp.dot(h, wg_ref[...], preferred_element_type=F32)
    val = jnp.dot(h[mid], wv_ref[...], preferred_element_type=F32)
    g_prev = pltpu.roll(gate, 1, axis=0)[mid]
    g_next = pltpu.roll(gate, ext - 1, axis=0)[mid]
    gc = cb_ref[...] + g_prev * cw_ref[0:1, :] + gate[mid] * cw_ref[1:2, :] + g_next * cw_ref[2:3, :]
    act = (gc * jax.nn.sigmoid(gc) * val).astype(BF16)
    out = x1[mid] + jnp.dot(act, wd_ref[...], preferred_element_type=F32)
    if final_norm:
        out = _rms(out, gf_ref[...])
    o_ref[...] = out


def _mix_ffn(x, acts, w_outs, g_ffn, w_up, conv_w, conv_b, w_down, g_final, *, tm, final_norm):
    b, s, d = x.shape
    per = tm // HALO
    last = s // HALO - 1
    wg = w_up[:, :D_FF].astype(BF16)
    wv = w_up[:, D_FF:].astype(BF16)
    wd = w_down.astype(BF16)

    def triple(width):
        return [
            pl.BlockSpec((None, HALO, width), lambda bi, i: (bi, jnp.maximum(i * per - 1, 0), 0)),
            pl.BlockSpec((None, tm, width), lambda bi, i: (bi, i, 0)),
            pl.BlockSpec((None, HALO, width), lambda bi, i: (bi, jnp.minimum((i + 1) * per, last), 0)),
        ]

    in_specs = triple(d)
    operands = [x, x, x]
    for a in acts:
        in_specs += triple(a.shape[-1])
        operands += [a, a, a]
    consts = list(w_outs) + [g_ffn.reshape(1, d), wg, wv, conv_w, conv_b.reshape(1, D_FF), wd,
                             g_final.reshape(1, d)]
    in_specs += [_const_spec(c.shape) for c in consts]
    return pl.pallas_call(
        functools.partial(_mix_ffn_kernel, tm=tm, n_acts=len(acts), final_norm=final_norm),
        grid=(b, s // tm),
        in_specs=in_specs,
        out_specs=pl.BlockSpec((None, tm, d), lambda bi, i: (bi, i, 0)),
        out_shape=jax.ShapeDtypeStruct((b, s, d), F32),
        compiler_params=_params("parallel", "parallel"),
        name="mix_ffn_final" if final_norm else "mix_ffn",
    )(*operands, *consts)


def _rope_angles(pos, dim):
    inv_freq = ROPE_THETA ** (-jnp.arange(0, dim, 2, dtype=F32) / dim)
    ang = pos.astype(F32)[:, None] * inv_freq[None, :]
    return jnp.cos(ang), jnp.sin(ang)


def _rope_tables(s):
    t = jnp.arange(s)
    n_rows = s // GRID_W
    cos_r, sin_r = (jnp.repeat(a, GRID_W, axis=0) for a in _rope_angles(jnp.arange(n_rows), HEAD_DIM // 2))
    cos_c, sin_c = (jnp.tile(a, (n_rows, 1)) for a in _rope_angles(jnp.arange(GRID_W), HEAD_DIM // 2))
    cos_ax = jnp.tile(jnp.concatenate([cos_r, cos_r, cos_c, cos_c], axis=1), (1, 2))
    sin_ax = jnp.tile(jnp.concatenate([sin_r, sin_r, sin_c, sin_c], axis=1), (1, 2))
    cos_t, sin_t = _rope_angles(t, ROPE_DIM)
    tail = jnp.zeros((s, LANES - NOPE_DIM - ROPE_DIM), F32)
    cos_l = jnp.concatenate([jnp.ones((s, NOPE_DIM), F32), cos_t, cos_t, tail], axis=1)
    sin_l = jnp.concatenate([jnp.zeros((s, NOPE_DIM), F32), sin_t, sin_t, tail], axis=1)
    return cos_ax, sin_ax, cos_l, sin_l


TM = 512
TQ_SHARED = 512
TQ_MHA = 1024
TK = 1024
TQ_WINDOW = 256


def _trunk(x, p, tables):
    cos_ax, sin_ax, cos_l, sin_l = tables
    s = x.shape[1]
    tm = min(TM, s)
    qa, ka, va, qb, kb, vb = _inproj_even(x, p["norm_mix"][0], p["e_w_in"][0], p["e_q_gain"][0],
                                          p["e_k_gain"][0], cos_ax, sin_ax, tm=tm)
    oa = _flash_attention(qa, ka, va, tq=min(TQ_SHARED, s), tk=min(TK, s // 8))
    ob = _window_attention(qb, kb, vb, p["e_sink"][0], tq=min(TQ_WINDOW, s))
    w_out = p["e_w_out"][0].astype(BF16)
    n_a = A_HEADS * HEAD_DIM
    x = _mix_ffn(x, [oa, ob], [w_out[:n_a], w_out[n_a:]], p["norm_ffn"][0], p["f_w_up"][0], p["f_conv_w"][0],
                 p["f_conv_b"][0], p["f_w_down"][0], p["norm_final"], tm=tm, final_norm=False)
    q, k, v = _inproj_latent(x, p["norm_mix"][1], p["o_w_down"][0], p["o_q_gain"][0], p["o_kv_gain"][0],
                             p["o_w_uq"][0], p["o_w_ukv"][0], cos_l, sin_l, tm=tm)
    o = _flash_attention(q, k, v, tq=min(TQ_MHA, s), tk=min(TK, s // 8))
    x = _mix_ffn(x, [o], [p["o_w_out"][0].astype(BF16)], p["norm_ffn"][1], p["f_w_up"][1], p["f_conv_w"][1],
                 p["f_conv_b"][1], p["f_w_down"][1], p["norm_final"], tm=tm, final_norm=True)
    return x


def kernel(x_prompt, x_sample, norm_mix, norm_ffn, norm_final, e_w_in, e_q_gain, e_k_gain, e_sink, e_w_out, o_w_down, o_q_gain, o_kv_gain, o_w_uq, o_w_ukv, o_w_out, f_w_up, f_conv_w, f_conv_b, f_w_down):
    p = dict(norm_mix=norm_mix, norm_ffn=norm_ffn, norm_final=norm_final, e_w_in=e_w_in, e_q_gain=e_q_gain,
             e_k_gain=e_k_gain, e_sink=e_sink, e_w_out=e_w_out, o_w_down=o_w_down, o_q_gain=o_q_gain,
             o_kv_gain=o_kv_gain, o_w_uq=o_w_uq, o_w_ukv=o_w_ukv, o_w_out=o_w_out, f_w_up=f_w_up,
             f_conv_w=f_conv_w, f_conv_b=f_conv_b, f_w_down=f_w_down)
    tables = _rope_tables(x_prompt.shape[1])
    return _trunk(x_prompt, p, tables), _trunk(x_sample, p, tables)
```
